```python
import math
import jax, jax.numpy as jnp
from jax import lax
import numpy as np

D_MODEL = 1024
BATCH = 4
SEQ = 4096
DEPTH = 4

GRID_W = 64
EPS = 1e-6
NA_HEADS = 8
NA_HEAD_DIM = 64
NA_WIN_R = 8
NA_WIN_C = 16
NA_QCOLS = 16
NA_KCOLS = 2 * NA_WIN_C
GLA_HEADS = 4
GLA_DK = 64
GLA_DV = 128
GLA_GATE_RANK = 16
GLA_GATE_TAU = 16.0
GLA_CHUNK = 64
MLA_HEADS = 4
MLA_Q_RANK = 256
MLA_KV_RANK = 256
MLA_NOPE = 128
MLA_ROPE = 64
MLA_V = 128
MLA_QBLOCK = 128
ROPE_THETA = 10000.0
D_FF = 2816
N_BRANCH = 3
NA_W = NA_HEADS * NA_HEAD_DIM
GLA_QK_W = GLA_HEADS * GLA_DK
GLA_V_W = GLA_HEADS * GLA_DV
MLA_QK_HEAD = MLA_NOPE + MLA_ROPE
MLA_V_W = MLA_HEADS * MLA_V
IN_SPLITS = (NA_W, NA_W, NA_W,
             GLA_QK_W, GLA_QK_W, GLA_V_W, GLA_V_W, GLA_GATE_RANK, GLA_GATE_RANK,
             MLA_Q_RANK, MLA_KV_RANK, MLA_ROPE,
             N_BRANCH * D_MODEL)
D_IN = 6752

kernel_name = "hybrid_na_gla_mla_macaron_encoder"


def rms_norm(x, g):
    xf = x.astype(jnp.float32)
    y = xf * lax.rsqrt(jnp.mean(xf * xf, axis=-1, keepdims=True) + EPS)
    return (y * g.astype(jnp.float32)).astype(x.dtype)


def swiglu(h, w1, w3, w2):
    return (jax.nn.silu(h @ w1) * (h @ w3)) @ w2


def split_cols(z, sizes):
    out, start = [], 0
    for n in sizes:
        out.append(z[..., start:start + n])
        start += n
    return out


def neighborhood_attention(q, k, v, rpb):
    B, S, H, d = q.shape
    rows = S // GRID_W
    win_r = min(NA_WIN_R, rows)
    nj = GRID_W // NA_QCOLS
    r = np.arange(rows)
    r0 = np.clip(r - win_r // 2, 0, rows - win_r)
    ridx = r0[:, None] + np.arange(win_r)
    j = np.arange(nj)
    k0 = np.clip(j * NA_QCOLS - NA_WIN_C // 2, 0, GRID_W - NA_KCOLS)
    cidx = k0[:, None] + np.arange(NA_KCOLS)
    qc = j[:, None] * NA_QCOLS + np.arange(NA_QCOLS)
    c0 = np.clip(qc - NA_WIN_C // 2, 0, GRID_W - NA_WIN_C)
    col_ok = (cidx[:, None, :] >= c0[..., None]) & (cidx[:, None, :] < c0[..., None] + NA_WIN_C)
    mask = np.broadcast_to(col_ok[:, :, None, :], (nj, NA_QCOLS, win_r, NA_KCOLS)).reshape(nj, NA_QCOLS, win_r * NA_KCOLS)
    dr = ridx - r[:, None] + (NA_WIN_R - 1)
    dc = np.clip(cidx[:, None, :] - qc[:, :, None] + (NA_WIN_C - 1), 0, 2 * NA_WIN_C - 2)
    bias = rpb[:, dr[:, None, None, :, None], dc[None, :, :, None, :]]
    bias = bias.reshape(H, rows, nj, NA_QCOLS, win_r * NA_KCOLS)

    ri = ridx[:, None, :, None]
    ci = cidx[None, :, None, :]
    kg = k.reshape(B, rows, GRID_W, H, d)[:, ri, ci].reshape(B, rows, nj, win_r * NA_KCOLS, H, d)
    vg = v.reshape(B, rows, GRID_W, H, d)[:, ri, ci].reshape(B, rows, nj, win_r * NA_KCOLS, H, d)
    qg = q.reshape(B, rows, nj, NA_QCOLS, H, d)
    s = jnp.einsum('brjqhd,brjkhd->bhrjqk', qg, kg).astype(jnp.float32) * (d ** -0.5)
    s = jnp.where(mask, s + bias.astype(jnp.float32), -1e30)
    p = jax.nn.softmax(s, axis=-1).astype(v.dtype)
    o = jnp.einsum('bhrjqk,brjkhd->brjqhd', p, vg)
    return o.reshape(B, S, H * d)


def gla_chunked(q, k, v, g, strict):
    B, H, S, dk = q.shape
    dv = v.shape[-1]
    n = S // GLA_CHUNK
    q = q.reshape(B, H, n, GLA_CHUNK, dk)
    k = k.reshape(B, H, n, GLA_CHUNK, dk)
    g = g.reshape(B, H, n, GLA_CHUNK, dk)
    v = v.reshape(B, H, n, GLA_CHUNK, dv)
    b = jnp.cumsum(g, axis=3)
    b_last = b[:, :, :, -1:, :]
    qe = q * jnp.exp(b)
    ke = k * jnp.exp(-b)
    k_end = k * jnp.exp(b_last - b)
    tri = np.tril(np.ones((GLA_CHUNK, GLA_CHUNK), dtype=bool), -1 if strict else 0)
    a = jnp.where(tri, jnp.einsum('bhnid,bhnjd->bhnij', qe, ke), 0.0)
    o_intra = jnp.einsum('bhnij,bhnjv->bhniv', a, v)
    upd = jnp.einsum('bhncd,bhncv->bhndv', k_end, v)
    decay = jnp.exp(b_last[:, :, :, 0, :])

    def step(state, inp):
        dec, u = inp
        return dec[..., None] * state + u, state

    init = jnp.zeros((B, H, dk, dv), q.dtype)
    _, s_prev = lax.scan(step, init, (jnp.moveaxis(decay, 2, 0), jnp.moveaxis(upd, 2, 0)))
    s_prev = jnp.moveaxis(s_prev, 0, 2)
    o = o_intra + jnp.einsum('bhnid,bhndv->bhniv', qe, s_prev)
    return o.reshape(B, H, S, dv)


def gla_bidirectional(q, k, v, g_fwd, g_bwd):
    t = lambda a: jnp.swapaxes(a.astype(jnp.float32), 1, 2)
    q, k, v, g_fwd, g_bwd = t(q), t(k), t(v), t(g_fwd), t(g_bwd)
    q = q * (GLA_DK ** -0.5)
    flip = lambda a: a[:, :, ::-1]
    o_f = gla_chunked(q, k, v, g_fwd, False)
    o_b = flip(gla_chunked(flip(q), flip(k), flip(v), flip(g_bwd), True))
    return jnp.swapaxes(o_f + o_b, 1, 2)


def rope_tables(S):
    half = MLA_ROPE // 2
    inv = ROPE_THETA ** (-jnp.arange(half, dtype=jnp.float32) / half)
    ang = jnp.arange(S, dtype=jnp.float32)[:, None] * inv[None, :]
    return jnp.cos(ang), jnp.sin(ang)


def apply_rope(x, cos, sin):
    xf = x.astype(jnp.float32)
    x1, x2 = xf[..., :MLA_ROPE // 2], xf[..., MLA_ROPE // 2:]
    c, s = cos[None, :, None, :], sin[None, :, None, :]
    return jnp.concatenate([x1 * c - x2 * s, x1 * s + x2 * c], axis=-1).astype(x.dtype)


def blocked_softmax_attention(q, k, v, scale):
    B, S, H, dq = q.shape
    nb = S // MLA_QBLOCK
    qb = jnp.moveaxis(q.reshape(B, nb, MLA_QBLOCK, H, dq), 1, 0)

    def one(qblk):
        s = jnp.einsum('bqhd,bkhd->bhqk', qblk, k).astype(jnp.float32) * scale
        p = jax.nn.softmax(s, axis=-1).astype(v.dtype)
        return jnp.einsum('bhqk,bkhd->bqhd', p, v)

    o = lax.map(one, qb)
    return jnp.moveaxis(o, 0, 1).reshape(B, S, H * v.shape[-1])


def mla_attention(c_q, c_kv, k_rope, cq_norm, ckv_norm, w_uq, w_ukv, q_norm, k_norm, cos, sin):
    B, S, _ = c_q.shape
    q = (rms_norm(c_q, cq_norm) @ w_uq).reshape(B, S, MLA_HEADS, MLA_QK_HEAD)
    kv = (rms_norm(c_kv, ckv_norm) @ w_ukv).reshape(B, S, MLA_HEADS, MLA_NOPE + MLA_V)
    k_nope, v = kv[..., :MLA_NOPE], kv[..., MLA_NOPE:]
    k_r = jnp.broadcast_to(k_rope[:, :, None, :], (B, S, MLA_HEADS, MLA_ROPE))
    k = jnp.concatenate([k_nope, k_r], axis=-1)
    q = rms_norm(q, q_norm)
    k = rms_norm(k, k_norm)
    q = jnp.concatenate([q[..., :MLA_NOPE], apply_rope(q[..., MLA_NOPE:], cos, sin)], axis=-1)
    k = jnp.concatenate([k[..., :MLA_NOPE], apply_rope(k[..., MLA_NOPE:], cos, sin)], axis=-1)
    return blocked_softmax_attention(q, k, v, MLA_QK_HEAD ** -0.5)


def setup_inputs(seed: int = 0) -> dict:
    key = jax.random.key(seed)
    ks = iter(jax.random.split(key, 32))
    L = DEPTH
    f32 = jnp.float32

    def w(shape, fan_in):
        return jax.random.normal(next(ks), shape, f32) * (fan_in ** -0.5)

    def gain(shape):
        return 1.0 + 0.05 * jax.random.normal(next(ks), shape, f32)

    def small(shape, scale, offset=0.0):
        return offset + scale * jax.random.normal(next(ks), shape, f32)

    return {
        "x": jax.random.normal(next(ks), (BATCH, SEQ, D_MODEL), f32),
        "ffn1_norm": gain((L, D_MODEL)),
        "ffn1_w1": w((L, D_MODEL, D_FF), D_MODEL),
        "ffn1_w3": w((L, D_MODEL, D_FF), D_MODEL),
        "ffn1_w2": w((L, D_FF, D_MODEL), D_FF),
        "mix_norm": gain((L, D_MODEL)),
        "w_in": w((L, D_MODEL, D_IN), D_MODEL),
        "na_q_norm": gain((L, NA_HEAD_DIM)),
        "na_k_norm": gain((L, NA_HEAD_DIM)),
        "na_rpb": small((L, NA_HEADS, 2 * NA_WIN_R - 1, 2 * NA_WIN_C - 1), 0.1),
        "gla_gf_up": w((L, GLA_GATE_RANK, GLA_QK_W), GLA_GATE_RANK),
        "gla_gf_bias": small((L, GLA_QK_W), 0.1, 2.0),
        "gla_gb_up": w((L, GLA_GATE_RANK, GLA_QK_W), GLA_GATE_RANK),
        "gla_gb_bias": small((L, GLA_QK_W), 0.1, 2.0),
        "gla_out_norm": gain((L, GLA_DV)),
        "mla_cq_norm": gain((L, MLA_Q_RANK)),
        "mla_ckv_norm": gain((L, MLA_KV_RANK)),
        "mla_w_uq": w((L, MLA_Q_RANK, MLA_HEADS * MLA_QK_HEAD), MLA_Q_RANK),
        "mla_w_ukv": w((L, MLA_KV_RANK, MLA_HEADS * (MLA_NOPE + MLA_V)), MLA_KV_RANK),
        "mla_q_norm": gain((L, MLA_QK_HEAD)),
        "mla_k_norm": gain((L, MLA_QK_HEAD)),
        "w_br_na": w((L, NA_W, D_MODEL), NA_W),
        "w_br_gla": w((L, GLA_V_W, D_MODEL), GLA_V_W),
        "w_br_mla": w((L, MLA_V_W, D_MODEL), MLA_V_W),
        "w_out": w((L, D_MODEL, D_MODEL), D_MODEL),
        "ffn2_norm": gain((L, D_MODEL)),
        "ffn2_w1": w((L, D_MODEL, D_FF), D_MODEL),
        "ffn2_w3": w((L, D_MODEL, D_FF), D_MODEL),
        "ffn2_w2": w((L, D_FF, D_MODEL), D_FF),
    }


def reference(x, ffn1_norm, ffn1_w1, ffn1_w3, ffn1_w2, mix_norm, w_in,
              na_q_norm, na_k_norm, na_rpb,
              gla_gf_up, gla_gf_bias, gla_gb_up, gla_gb_bias, gla_out_norm,
              mla_cq_norm, mla_ckv_norm, mla_w_uq, mla_w_ukv, mla_q_norm, mla_k_norm,
              w_br_na, w_br_gla, w_br_mla, w_out,
              ffn2_norm, ffn2_w1, ffn2_w3, ffn2_w2):
    B, S, D = x.shape
    cos, sin = rope_tables(S)
    for l in range(DEPTH):
        x = x + 0.5 * swiglu(rms_norm(x, ffn1_norm[l]), ffn1_w1[l], ffn1_w3[l], ffn1_w2[l])

        h = rms_norm(x, mix_norm[l])
        z = h @ w_in[l]
        (na_q, na_k, na_v, gq, gk, gv, gr, gfl, gbl, c_q, c_kv, k_rope, gates) = split_cols(z, IN_SPLITS)

        qa = rms_norm(na_q.reshape(B, S, NA_HEADS, NA_HEAD_DIM), na_q_norm[l])
        ka = rms_norm(na_k.reshape(B, S, NA_HEADS, NA_HEAD_DIM), na_k_norm[l])
        va = na_v.reshape(B, S, NA_HEADS, NA_HEAD_DIM)
        y_na = neighborhood_attention(qa, ka, va, na_rpb[l])

        g_f = jax.nn.log_sigmoid((gfl @ gla_gf_up[l] + gla_gf_bias[l]).astype(jnp.float32)) / GLA_GATE_TAU
        g_b = jax.nn.log_sigmoid((gbl @ gla_gb_up[l] + gla_gb_bias[l]).astype(jnp.float32)) / GLA_GATE_TAU
        o_gla = gla_bidirectional(gq.reshape(B, S, GLA_HEADS, GLA_DK), gk.reshape(B, S, GLA_HEADS, GLA_DK),
                                  gv.reshape(B, S, GLA_HEADS, GLA_DV),
                                  g_f.reshape(B, S, GLA_HEADS, GLA_DK), g_b.reshape(B, S, GLA_HEADS, GLA_DK))
        o_gla = rms_norm(o_gla, gla_out_norm[l]).astype(x.dtype).reshape(B, S, GLA_V_W)
        y_gla = o_gla * jax.nn.silu(gr)

        y_mla = mla_attention(c_q, c_kv, k_rope, mla_cq_norm[l], mla_ckv_norm[l], mla_w_uq[l], mla_w_ukv[l],
                              mla_q_norm[l], mla_k_norm[l], cos, sin)

        gt = jax.nn.sigmoid(gates.reshape(B, S, N_BRANCH, D))
        mixed = (gt[:, :, 0] * (y_na @ w_br_na[l])
                 + gt[:, :, 1] * (y_gla @ w_br_gla[l])
                 + gt[:, :, 2] * (y_mla @ w_br_mla[l]))
        x = x + mixed @ w_out[l]

        x = x + 0.5 * swiglu(rms_norm(x, ffn2_norm[l]), ffn2_w1[l], ffn2_w3[l], ffn2_w2[l])
    return x
```

```python
import functools

import numpy as np
import jax
import jax.numpy as jnp
from jax import lax
from jax.experimental import pallas as pl
from jax.experimental.pallas import tpu as pltpu

F32 = jnp.float32
BF16 = jnp.bfloat16

EPS = 1e-6
D_MODEL = 1024
D_FF = 2816
GRID_W = 64
NA_HEADS = 8
NA_HEAD_DIM = 64
NA_WIN_R = 8
NA_WIN_C = 16
NA_W = NA_HEADS * NA_HEAD_DIM
GLA_HEADS = 4
GLA_DK = 64
GLA_DV = 128
GLA_GATE_RANK = 16
GLA_GATE_TAU = 16.0
GLA_CHUNK = 64
GLA_QK_W = GLA_HEADS * GLA_DK
GLA_V_W = GLA_HEADS * GLA_DV
MLA_HEADS = 4
MLA_Q_RANK = 256
MLA_KV_RANK = 256
MLA_NOPE = 128
MLA_ROPE = 64
MLA_V = 128
MLA_QK_HEAD = MLA_NOPE + MLA_ROPE
MLA_SLAB = 256
MLA_V_W = MLA_HEADS * MLA_V
ROPE_THETA = 10000.0
N_BRANCH = 3

LANES = 128
C_NA = 0
C_GQ = 3 * NA_W
C_GK = C_GQ + GLA_QK_W
C_GV = C_GK + GLA_QK_W
C_GR = C_GV + GLA_V_W
C_GFB = C_GR + GLA_V_W
C_CQ = C_GFB + LANES
C_CKV = C_CQ + MLA_Q_RANK
C_KR = C_CKV + MLA_KV_RANK
C_GATE = C_KR + LANES
D_IN_PACKED = C_GATE + N_BRANCH * D_MODEL

VMEM_LIMIT = 56 * 1024 * 1024


def _cparams(*sem):
    return pltpu.CompilerParams(dimension_semantics=sem, vmem_limit_bytes=VMEM_LIMIT)


def _resident(shape):
    nd = len(shape)
    return pl.BlockSpec(shape, lambda *_: (0,) * nd, pipeline_mode=pl.Buffered(1))


def _dot(a, b):
    return jnp.dot(a, b, preferred_element_type=F32)


def _dot_nt(a, b):
    return lax.dot_general(a, b, (((1,), (1,)), ((), ())), preferred_element_type=F32)


def _dot_tn(a, b):
    return lax.dot_general(a, b, (((0,), (0,)), ((), ())), preferred_element_type=F32)


def _rms(x, g):
    ms = jnp.mean(x * x, axis=-1, keepdims=True)
    return x * lax.rsqrt(ms + EPS) * g


def _ffn_body(x_ref, g_ref, w1_ref, w3_ref, w2_ref, o_ref, *, ff_chunk):
    x = x_ref[...]
    h = _rms(x, g_ref[...]).astype(BF16)
    acc = None
    for f0 in range(0, D_FF, ff_chunk):
        a = _dot(h, w1_ref[:, f0:f0 + ff_chunk])
        b = _dot(h, w3_ref[:, f0:f0 + ff_chunk])
        u = (a * jax.nn.sigmoid(a) * b).astype(BF16)
        part = _dot(u, w2_ref[f0:f0 + ff_chunk, :])
        acc = part if acc is None else acc + part
    o_ref[...] = x + 0.5 * acc


def _ffn(x, g, w1, w3, w2, *, tm=512, ff_chunk=1408):
    n, d = x.shape
    return pl.pallas_call(
        functools.partial(_ffn_body, ff_chunk=ff_chunk),
        grid=(n // tm,),
        in_specs=[pl.BlockSpec((tm, d), lambda i: (i, 0)),
                  _resident((1, d)), _resident(w1.shape), _resident(w3.shape), _resident(w2.shape)],
        out_specs=pl.BlockSpec((tm, d), lambda i: (i, 0)),
        out_shape=jax.ShapeDtypeStruct((n, d), F32),
        compiler_params=_cparams("parallel"),
        name="ffn",
    )(x, g, w1, w3, w2)


def _inproj_body(x_ref, g_ref, w_ref, qg_ref, kg_ref, hsum_ref,
                 naq_ref, nak_ref, nav_ref, gq_ref, gk_ref, gv_ref, gr_ref, gfb_ref,
                 cq_ref, ckv_ref, kr_ref, gt_ref):
    h = _rms(x_ref[...], g_ref[...]).astype(BF16)

    def proj(c0, n):
        return _dot(h, w_ref[:, c0:c0 + n])

    hsum = hsum_ref[...]

    def head_norm(t, gain):
        sq = t * t
        hi = sq.astype(BF16)
        lo = (sq - hi.astype(F32)).astype(BF16)
        ss = _dot(hi, hsum) + _dot(lo, hsum)
        return t * lax.rsqrt(ss * (1.0 / NA_HEAD_DIM) + EPS) * gain

    naq_ref[...] = head_norm(proj(C_NA, NA_W), qg_ref[...]).astype(BF16)
    nak_ref[...] = head_norm(proj(C_NA + NA_W, NA_W), kg_ref[...]).astype(BF16)
    nav_ref[...] = proj(C_NA + 2 * NA_W, NA_W).astype(BF16)
    gq_ref[...] = proj(C_GQ, GLA_QK_W)
    gk_ref[...] = proj(C_GK, GLA_QK_W)
    gv_ref[...] = proj(C_GV, GLA_V_W)
    gr_ref[...] = proj(C_GR, GLA_V_W)
    gfb_ref[...] = proj(C_GFB, LANES)
    cq_ref[...] = proj(C_CQ, MLA_Q_RANK)
    ckv_ref[...] = proj(C_CKV, MLA_KV_RANK)
    kr_ref[...] = proj(C_KR, LANES)
    for i in range(N_BRANCH):
        c0 = C_GATE + i * D_MODEL
        gt_ref[:, i * D_MODEL:(i + 1) * D_MODEL] = jax.nn.sigmoid(proj(c0, D_MODEL)).astype(BF16)


def _inproj(x, g, w, qg, kg, hsum, *, tm=512):
    n, d = x.shape
    widths = [(NA_W, BF16), (NA_W, BF16), (NA_W, BF16),
              (GLA_QK_W, F32), (GLA_QK_W, F32), (GLA_V_W, F32), (GLA_V_W, F32), (LANES, F32),
              (MLA_Q_RANK, F32), (MLA_KV_RANK, F32), (LANES, F32),
              (N_BRANCH * D_MODEL, BF16)]
    return pl.pallas_call(
        _inproj_body,
        grid=(n // tm,),
        in_specs=[pl.BlockSpec((tm, d), lambda i: (i, 0)),
                  _resident((1, d)), _resident(w.shape), _resident(qg.shape), _resident(kg.shape),
                  _resident(hsum.shape)],
        out_specs=[pl.BlockSpec((tm, c), lambda i: (i, 0)) for c, _ in widths],
        out_shape=[jax.ShapeDtypeStruct((n, c), dt) for c, dt in widths],
        compiler_params=_cparams("parallel"),
        name="inproj",
    )(x, g, w, qg, kg, hsum)


def _na_body(q_ref, k_ref, v_ref, bias_ref, o_ref, *, rb, rows):
    kw_tokens = NA_WIN_R * GRID_W
    lane = lax.broadcasted_iota(jnp.int32, (1, LANES), 1)
    first_head = lane < NA_HEAD_DIM
    r_base = pl.program_id(1) * rb
    for i in range(rb):
        r = r_base + i
        r0 = jnp.clip(r - NA_WIN_R // 2, 0, rows - NA_WIN_R)
        delta = r - r0
        start = pl.multiple_of(r0 * GRID_W, GRID_W)
        kw = k_ref[0, pl.ds(start, kw_tokens), :]
        vw = v_ref[0, pl.ds(start, kw_tokens), :]
        q = q_ref[0, i * GRID_W:(i + 1) * GRID_W, :]
        for p in range(NA_HEADS // 2):
            sl = slice(p * LANES, (p + 1) * LANES)
            qs, ks, vs = q[:, sl], kw[:, sl], vw[:, sl]
            res = []
            for hh in range(2):
                qm = jnp.where(first_head if hh == 0 else ~first_head, qs, jnp.zeros_like(qs))
                s = _dot_nt(qm, ks) + bias_ref[delta, 2 * p + hh]
                mx = jnp.max(s, axis=-1, keepdims=True)
                e = jnp.exp(s - mx)
                l = jnp.sum(e, axis=-1, keepdims=True)
                res.append(_dot(e.astype(BF16), vs) / l)
            o_ref[0, i * GRID_W:(i + 1) * GRID_W, sl] = jnp.where(first_head, res[0], res[1]).astype(BF16)


def _na_attention(q, k, v, bias, *, rb=8):
    b, s, w = q.shape
    rows = s // GRID_W
    assert rows >= NA_WIN_R and rows % rb == 0
    return pl.pallas_call(
        functools.partial(_na_body, rb=rb, rows=rows),
        grid=(b, rows // rb),
        in_specs=[pl.BlockSpec((1, rb * GRID_W, w), lambda bi, ri: (bi, ri, 0)),
                  pl.BlockSpec((1, s, w), lambda bi, ri: (bi, 0, 0)),
                  pl.BlockSpec((1, s, w), lambda bi, ri: (bi, 0, 0)),
                  _resident(bias.shape)],
        out_specs=pl.BlockSpec((1, rb * GRID_W, w), lambda bi, ri: (bi, ri, 0)),
        out_shape=jax.ShapeDtypeStruct((b, s, w), BF16),
        compiler_params=_cparams("parallel", "arbitrary"),
        name="na_attn",
    )(q, k, v, bias)


def _na_bias_table(rpb):
    qc = np.arange(GRID_W)
    kc = np.arange(GRID_W)
    dc = np.clip(kc[None, :] - qc[:, None] + (NA_WIN_C - 1), 0, 2 * NA_WIN_C - 2)
    c0 = np.clip(qc - NA_WIN_C // 2, 0, GRID_W - NA_WIN_C)
    ok = (kc[None, :] >= c0[:, None]) & (kc[None, :] < c0[:, None] + NA_WIN_C)
    delta = np.arange(NA_WIN_R)
    j = np.arange(NA_WIN_R)
    dr = j[None, :] - delta[:, None] + (NA_WIN_R - 1)
    tab = rpb[:, :, dr[:, :, None, None], dc[None, None, :, :]]
    tab = jnp.where(ok[None, None, None, None], tab.astype(F32), -1e30)
    tab = jnp.transpose(tab, (0, 2, 1, 4, 3, 5))
    return tab.reshape(tab.shape[0], NA_WIN_R, NA_HEADS, GRID_W, NA_WIN_R * GRID_W)


def _gla_body(gq_ref, gk_ref, gv_ref, gfb_ref, gr_ref, up_ref, gbias_ref, onorm_ref, tri_ref,
              y_ref, ob_ref, st_ref, *, nb, tb):
    t = pl.program_id(1)
    nc = tb // GLA_CHUNK
    lane = lax.broadcasted_iota(jnp.int32, (1, LANES), 1)
    first_head = lane < GLA_DK
    ci = lax.broadcasted_iota(jnp.int32, (GLA_CHUNK, GLA_CHUNK), 0)
    cj = lax.broadcasted_iota(jnp.int32, (GLA_CHUNK, GLA_CHUNK), 1)

    @pl.when((t == 0) | (t == nb))
    def _():
        st_ref[...] = jnp.zeros_like(st_ref)

    def scan_block(bwd):
        blk = (nb - 1 - t) if bwd else (t - nb)
        row0 = pl.multiple_of(blk * tb, tb)
        gsl = slice(GLA_QK_W, 2 * GLA_QK_W) if bwd else slice(0, GLA_QK_W)
        x = _dot(gfb_ref[0].astype(BF16), up_ref[:, gsl]) + gbias_ref[:, gsl]
        g = (jnp.minimum(x, 0.0) - jnp.log1p(jnp.exp(-jnp.abs(x)))) * (1.0 / GLA_GATE_TAU)
        g_hi = g.astype(BF16)
        g_lo = (g - g_hi.astype(F32)).astype(BF16)
        tri = tri_ref[1 if bwd else 0]
        cum = _dot(tri, g_hi) + _dot(tri, g_lo)
        k = gk_ref[0]
        qe = (gq_ref[0] * (GLA_DK ** -0.5) * jnp.exp(cum)).astype(BF16)
        ke = (k * jnp.exp(-cum)).astype(BF16)
        v = gv_ref[0].astype(BF16)
        keep = (cj > ci) if bwd else (cj <= ci)
        for c in (range(nc - 1, -1, -1) if bwd else range(nc)):
            rs = slice(c * GLA_CHUNK, (c + 1) * GLA_CHUNK)
            edge = c * GLA_CHUNK if bwd else (c + 1) * GLA_CHUNK - 1
            cum_end = cum[edge:edge + 1, :]
            k_end = (k[rs] * jnp.exp(cum_end - cum[rs])).astype(BF16)
            decay = jnp.exp(cum_end)
            for h in range(GLA_HEADS):
                ps = slice((h // 2) * LANES, (h // 2 + 1) * LANES)
                vs = slice(h * GLA_DV, (h + 1) * GLA_DV)
                mine = first_head if h % 2 == 0 else ~first_head
                qm = jnp.where(mine, qe[rs, ps], jnp.zeros((GLA_CHUNK, LANES), BF16))
                a = jnp.where(keep, _dot_nt(qm, ke[rs, ps]), 0.0).astype(BF16)
                vh = v[rs, vs]
                st = st_ref[h]
                o = _dot(a, vh) + _dot_nt(qm, st.astype(BF16))
                st_ref[h] = decay[:, ps] * st + _dot_tn(vh, k_end[:, ps])
                orow = pl.ds(row0 + c * GLA_CHUNK, GLA_CHUNK)
                if bwd:
                    ob_ref[orow, vs] = o
                else:
                    ob_ref[orow, vs] = ob_ref[orow, vs] + o
        return row0

    @pl.when(t < nb)
    def _():
        scan_block(True)

    @pl.when(t >= nb)
    def _():
        row0 = scan_block(False)
        tot = ob_ref[pl.ds(row0, tb), :]
        gr = gr_ref[0]
        for h in range(GLA_HEADS):
            vs = slice(h * GLA_DV, (h + 1) * GLA_DV)
            y = _rms(tot[:, vs], onorm_ref[:, vs])
            gate = gr[:, vs]
            y_ref[0, :, vs] = (y * (gate * jax.nn.sigmoid(gate))).astype(BF16)


def _gla(gq, gk, gv, gfb, gr, up, gbias, onorm, tri, *, tb=512):
    b, s, _ = gq.shape
    nb = s // tb
    assert tri.shape == (2, tb, tb)

    def scan_idx(bi, t):
        return (bi, jnp.where(t < nb, nb - 1 - t, t - nb), 0)

    def fwd_idx(bi, t):
        return (bi, jnp.maximum(t - nb, 0), 0)

    return pl.pallas_call(
        functools.partial(_gla_body, nb=nb, tb=tb),
        grid=(b, 2 * nb),
        in_specs=[pl.BlockSpec((1, tb, GLA_QK_W), scan_idx),
                  pl.BlockSpec((1, tb, GLA_QK_W), scan_idx),
                  pl.BlockSpec((1, tb, GLA_V_W), scan_idx),
                  pl.BlockSpec((1, tb, LANES), scan_idx),
                  pl.BlockSpec((1, tb, GLA_V_W), fwd_idx),
                  _resident(up.shape), _resident(gbias.shape), _resident(onorm.shape),
                  _resident(tri.shape)],
        out_specs=pl.BlockSpec((1, tb, GLA_V_W), fwd_idx),
        out_shape=jax.ShapeDtypeStruct((b, s, GLA_V_W), BF16),
        scratch_shapes=[pltpu.VMEM((s, GLA_V_W), F32),
                        pltpu.VMEM((GLA_HEADS, GLA_DV, LANES), F32)],
        compiler_params=_cparams("arbitrary", "arbitrary"),
        name="gla",
    )(gq, gk, gv, gfb, gr, up, gbias, onorm, tri)


def _gla_tri(tb):
    i = np.arange(tb)
    same = (i[:, None] // GLA_CHUNK) == (i[None, :] // GLA_CHUNK)
    lower = same & (i[None, :] <= i[:, None])
    upper = same & (i[None, :] >= i[:, None])
    return jnp.asarray(np.stack([lower, upper]).astype(np.float32), BF16)


def _mla_prep_body(cq_ref, ckv_ref, kr_ref, cqg_ref, ckvg_ref, wuq_ref, wukv_ref,
                   qg_ref, kgn_ref, kgr_ref, cos_ref, sa_ref, sb_ref,
                   q_ref, k_ref, v_ref):
    cos, sa, sb = cos_ref[...], sa_ref[...], sb_ref[...]

    def rope(slab):
        return slab * cos + pltpu.roll(slab, LANES - MLA_ROPE // 2, 1) * sa + pltpu.roll(slab, MLA_ROPE // 2, 1) * sb

    inv_d = 1.0 / MLA_QK_HEAD
    q = _dot(_rms(cq_ref[...], cqg_ref[...]).astype(BF16), wuq_ref[...])
    for h in range(MLA_HEADS):
        qh = q[:, h * MLA_SLAB:(h + 1) * MLA_SLAB]
        ss = jnp.sum(qh * qh, axis=-1, keepdims=True) * inv_d
        qn = qh * lax.rsqrt(ss + EPS) * qg_ref[...]
        q_ref[:, h * MLA_SLAB:h * MLA_SLAB + LANES] = qn[:, :LANES].astype(BF16)
        q_ref[:, h * MLA_SLAB + LANES:(h + 1) * MLA_SLAB] = rope(qn[:, LANES:]).astype(BF16)

    kv = _dot(_rms(ckv_ref[...], ckvg_ref[...]).astype(BF16), wukv_ref[...])
    kr = kr_ref[...]
    ss_r = jnp.sum(kr * kr, axis=-1, keepdims=True)
    kr_rot = rope(kr * kgr_ref[...])
    for h in range(MLA_HEADS):
        kn = kv[:, h * MLA_NOPE:(h + 1) * MLA_NOPE]
        ss = (jnp.sum(kn * kn, axis=-1, keepdims=True) + ss_r) * inv_d
        inv = lax.rsqrt(ss + EPS)
        k_ref[:, h * MLA_SLAB:h * MLA_SLAB + LANES] = (kn * inv * kgn_ref[...]).astype(BF16)
        k_ref[:, h * MLA_SLAB + LANES:(h + 1) * MLA_SLAB] = (kr_rot * inv).astype(BF16)
    v_ref[...] = kv[:, MLA_HEADS * MLA_NOPE:].astype(BF16)


def _mla_prep(cq, ckv, kr, cqg, ckvg, wuq, wukv, qg, kgn, kgr, cos, sa, sb, *, seq, tm=512):
    n = cq.shape[0]
    pos_blocks = seq // tm
    tok = lambda c: pl.BlockSpec((tm, c), lambda i: (i, 0))
    pos = pl.BlockSpec((tm, LANES), lambda i: (i % pos_blocks, 0))
    return pl.pallas_call(
        _mla_prep_body,
        grid=(n // tm,),
        in_specs=[tok(MLA_Q_RANK), tok(MLA_KV_RANK), tok(LANES),
                  _resident(cqg.shape), _resident(ckvg.shape), _resident(wuq.shape), _resident(wukv.shape),
                  _resident(qg.shape), _resident(kgn.shape), _resident(kgr.shape), pos, pos, pos],
        out_specs=[tok(MLA_HEADS * MLA_SLAB), tok(MLA_HEADS * MLA_SLAB), tok(MLA_V_W)],
        out_shape=[jax.ShapeDtypeStruct((n, MLA_HEADS * MLA_SLAB), BF16),
                   jax.ShapeDtypeStruct((n, MLA_HEADS * MLA_SLAB), BF16),
                   jax.ShapeDtypeStruct((n, MLA_V_W), BF16)],
        compiler_params=_cparams("parallel"),
        name="mla_prep",
    )(cq, ckv, kr, cqg, ckvg, wuq, wukv, qg, kgn, kgr, cos, sa, sb)


def _mla_attn_body(q_ref, k_ref, v_ref, o_ref, *, tk, scale):
    q = q_ref[0]
    tq = q.shape[0]
    s_len = k_ref.shape[1]
    m = jnp.full((tq, 1), -jnp.inf, F32)
    l = jnp.zeros((tq, 1), F32)
    acc = jnp.zeros((tq, MLA_V), F32)
    for j in range(s_len // tk):
        kb = k_ref[0, j * tk:(j + 1) * tk, :]
        vb = v_ref[0, j * tk:(j + 1) * tk, :]
        s = _dot_nt(q, kb) * scale
        m_new = jnp.maximum(m, jnp.max(s, axis=-1, keepdims=True))
        alpha = jnp.exp(m - m_new)
        p = jnp.exp(s - m_new)
        l = alpha * l + jnp.sum(p, axis=-1, keepdims=True)
        acc = alpha * acc + _dot(p.astype(BF16), vb)
        m = m_new
    o_ref[0] = (acc / l).astype(BF16)


def _mla_attention(q, k, v, *, tq=512, tk=512):
    b, s, _ = q.shape
    return pl.pallas_call(
        functools.partial(_mla_attn_body, tk=tk, scale=MLA_QK_HEAD ** -0.5),
        grid=(b, MLA_HEADS, s // tq),
        in_specs=[pl.BlockSpec((1, tq, MLA_SLAB), lambda bi, h, qi: (bi, qi, h)),
                  pl.BlockSpec((1, s, MLA_SLAB), lambda bi, h, qi: (bi, 0, h)),
                  pl.BlockSpec((1, s, MLA_V), lambda bi, h, qi: (bi, 0, h))],
        out_specs=pl.BlockSpec((1, tq, MLA_V), lambda bi, h, qi: (bi, qi, h)),
        out_shape=jax.ShapeDtypeStruct((b, s, MLA_V_W), BF16),
        compiler_params=_cparams("parallel", "parallel", "arbitrary"),
        name="mla_attn",
    )(q, k, v)


def _merge_body(x_ref, yna_ref, ygla_ref, ymla_ref, gt_ref, wbr_ref, wout_ref, o_ref):
    mixed = None
    for i, y_ref in enumerate((yna_ref, ygla_ref, ymla_ref)):
        br = _dot(y_ref[...], wbr_ref[i])
        term = gt_ref[:, i * D_MODEL:(i + 1) * D_MODEL].astype(F32) * br
        mixed = term if mixed is None else mixed + term
    o_ref[...] = x_ref[...] + _dot(mixed.astype(BF16), wout_ref[...])


def _merge(x, yna, ygla, ymla, gt, wbr, wout, *, tm=512):
    n, d = x.shape
    tok = lambda c: pl.BlockSpec((tm, c), lambda i: (i, 0))
    return pl.pallas_call(
        _merge_body,
        grid=(n // tm,),
        in_specs=[tok(d), tok(NA_W), tok(GLA_V_W), tok(MLA_V_W), tok(N_BRANCH * d),
                  _resident(wbr.shape), _resident(wout.shape)],
        out_specs=tok(d),
        out_shape=jax.ShapeDtypeStruct((n, d), F32),
        compiler_params=_cparams("parallel"),
        name="merge",
    )(x, yna, ygla, ymla, gt, wbr, wout)


def _rope_tables(s):
    half = MLA_ROPE // 2
    inv = ROPE_THETA ** (-jnp.arange(half, dtype=F32) / half)
    ang = jnp.arange(s, dtype=F32)[:, None] * inv[None, :]
    cos, sin = jnp.cos(ang), jnp.sin(ang)
    z = jnp.zeros((s, half), F32)
    pad = jnp.zeros((s, LANES - MLA_ROPE), F32)
    return (jnp.concatenate([cos, cos, pad], axis=1),
            jnp.concatenate([-sin, z, pad], axis=1),
            jnp.concatenate([z, sin, pad], axis=1))


def _pack_w_in(w_in):
    nl, d, _ = w_in.shape
    gfb_end = 3 * NA_W + 2 * GLA_QK_W + 2 * GLA_V_W + 2 * GLA_GATE_RANK
    kr_end = gfb_end + MLA_Q_RANK + MLA_KV_RANK + MLA_ROPE
    z = lambda c: jnp.zeros((nl, d, c), w_in.dtype)
    return jnp.concatenate([w_in[:, :, :gfb_end], z(LANES - 2 * GLA_GATE_RANK),
                            w_in[:, :, gfb_end:kr_end], z(LANES - MLA_ROPE),
                            w_in[:, :, kr_end:]], axis=2).astype(BF16)


def kernel(x, ffn1_norm, ffn1_w1, ffn1_w3, ffn1_w2, mix_norm, w_in, na_q_norm, na_k_norm, na_rpb, gla_gf_up, gla_gf_bias, gla_gb_up, gla_gb_bias, gla_out_norm, mla_cq_norm, mla_ckv_norm, mla_w_uq, mla_w_ukv, mla_q_norm, mla_k_norm, w_br_na, w_br_gla, w_br_mla, w_out, ffn2_norm, ffn2_w1, ffn2_w3, ffn2_w2):
    b, s, d = x.shape
    nl = w_in.shape[0]
    n = b * s
    gla_tb = 512

    row = lambda a: a[:, None, :].astype(F32)
    f1 = (row(ffn1_norm), ffn1_w1.astype(BF16), ffn1_w3.astype(BF16), ffn1_w2.astype(BF16))
    f2 = (row(ffn2_norm), ffn2_w1.astype(BF16), ffn2_w3.astype(BF16), ffn2_w2.astype(BF16))
    mixg = row(mix_norm)
    w_in_p = _pack_w_in(w_in)
    na_qg = row(jnp.tile(na_q_norm, (1, NA_HEADS)) * (NA_HEAD_DIM ** -0.5))
    na_kg = row(jnp.tile(na_k_norm, (1, NA_HEADS)))
    head_of = np.arange(NA_W) // NA_HEAD_DIM
    hsum = jnp.asarray((head_of[:, None] == head_of[None, :]).astype(np.float32), BF16)
    na_bias = _na_bias_table(na_rpb)

    up = jnp.zeros((nl, LANES, 2 * GLA_QK_W), F32)
    up = up.at[:, :GLA_GATE_RANK, :GLA_QK_W].set(gla_gf_up)
    up = up.at[:, GLA_GATE_RANK:2 * GLA_GATE_RANK, GLA_QK_W:].set(gla_gb_up).astype(BF16)
    gbias = row(jnp.concatenate([gla_gf_bias, gla_gb_bias], axis=1))
    onorm = row(jnp.tile(gla_out_norm, (1, GLA_HEADS)))
    tri = _gla_tri(gla_tb)

    cqg, ckvg = row(mla_cq_norm), row(mla_ckv_norm)
    wuq = mla_w_uq.reshape(nl, MLA_Q_RANK, MLA_HEADS, MLA_QK_HEAD)
    wuq = jnp.pad(wuq, ((0, 0), (0, 0), (0, 0), (0, MLA_SLAB - MLA_QK_HEAD)))
    wuq = wuq.reshape(nl, MLA_Q_RANK, MLA_HEADS * MLA_SLAB).astype(BF16)
    wukv = mla_w_ukv.reshape(nl, MLA_KV_RANK, MLA_HEADS, 2, MLA_NOPE)
    wukv = jnp.swapaxes(wukv, 2, 3).reshape(nl, MLA_KV_RANK, 2 * MLA_HEADS * MLA_NOPE).astype(BF16)
    mqg = row(jnp.pad(mla_q_norm, ((0, 0), (0, MLA_SLAB - MLA_QK_HEAD))))
    mkgn = row(mla_k_norm[:, :MLA_NOPE])
    mkgr = row(jnp.pad(mla_k_norm[:, MLA_NOPE:], ((0, 0), (0, LANES - MLA_ROPE))))
    cos, sa, sb = _rope_tables(s)

    wbr = jnp.stack([w_br_na, w_br_gla, w_br_mla], axis=1).astype(BF16)
    wout = w_out.astype(BF16)

    xf = x.reshape(n, d).astype(F32)
    seq3 = lambda a: a.reshape(b, s, a.shape[-1])
    for l in range(nl):
        xf = _ffn(xf, f1[0][l], f1[1][l], f1[2][l], f1[3][l])
        (naq, nak, nav, gq, gk, gv, gr, gfb, cq, ckv, kr, gt) = _inproj(
            xf, mixg[l], w_in_p[l], na_qg[l], na_kg[l], hsum)
        y_na = _na_attention(seq3(naq), seq3(nak), seq3(nav), na_bias[l])
        y_gla = _gla(seq3(gq), seq3(gk), seq3(gv), seq3(gfb), seq3(gr),
                     up[l], gbias[l], onorm[l], tri, tb=gla_tb)
        mq, mk, mv = _mla_prep(cq, ckv, kr, cqg[l], ckvg[l], wuq[l], wukv[l],
                               mqg[l], mkgn[l], mkgr[l], cos, sa, sb, seq=s)
        y_mla = _mla_attention(seq3(mq), seq3(mk), seq3(mv))
        xf = _merge(xf, y_na.reshape(n, NA_W), y_gla.reshape(n, GLA_V_W), y_mla.reshape(n, MLA_V_W),
                    gt, wbr[l], wout[l])
        xf = _ffn(xf, f2[0][l], f2[1][l], f2[2][l], f2[3][l])
    return xf.reshape(b, s, d).astype(x.dtype)
```

```python
import functools
import math

import numpy as np
import jax
import jax.numpy as jnp
from jax import lax
from jax.experimental import pallas as pl
from jax.experimental.pallas import tpu as pltpu

F32 = jnp.float32
BF16 = jnp.bfloat16

EPS = 1e-6
D_MODEL = 1024
D_FF = 2816
GRID_W = 64
NA_HEADS = 8
NA_HEAD_DIM = 64
NA_WIN_R = 8
NA_WIN_C = 16
NA_W = NA_HEADS * NA_HEAD_DIM
GLA_HEADS = 4
GLA_DK = 64
GLA_DV = 128
GLA_GATE_RANK = 16
GLA_GATE_TAU = 16.0
GLA_CHUNK = 64
GLA_QK_W = GLA_HEADS * GLA_DK
GLA_V_W = GLA_HEADS * GLA_DV
MLA_HEADS = 4
MLA_Q_RANK = 256
MLA_KV_RANK = 256
MLA_NOPE = 128
MLA_ROPE = 64
MLA_V = 128
MLA_QK_HEAD = MLA_NOPE + MLA_ROPE
MLA_SLAB = 256
MLA_V_W = MLA_HEADS * MLA_V
ROPE_THETA = 10000.0
N_BRANCH = 3

LANES = 128
C_NA = 0
C_GQ = 3 * NA_W
C_GK = C_GQ + GLA_QK_W
C_GV = C_GK + GLA_QK_W
C_GR = C_GV + GLA_V_W
C_GFB = C_GR + GLA_V_W
C_CQ = C_GFB + LANES
C_CKV = C_CQ + MLA_Q_RANK
C_KR = C_CKV + MLA_KV_RANK
C_GATE = C_KR + LANES
D_IN_PACKED = C_GATE + N_BRANCH * D_MODEL

VMEM_LIMIT = 56 * 1024 * 1024


def _cparams(*sem):
    return pltpu.CompilerParams(dimension_semantics=sem, vmem_limit_bytes=VMEM_LIMIT)


def _resident(shape):
    nd = len(shape)
    return pl.BlockSpec(shape, lambda *_: (0,) * nd, pipeline_mode=pl.Buffered(1))


def _layer(arr, l):
    nd = arr.ndim - 1
    return pl.BlockSpec((None,) + tuple(arr.shape[1:]), lambda *_: (l,) + (0,) * nd,
                        pipeline_mode=pl.Buffered(1))


def _dot(a, b):
    return jnp.dot(a, b, preferred_element_type=F32)


def _dot_nt(a, b):
    return lax.dot_general(a, b, (((1,), (1,)), ((), ())), preferred_element_type=F32)


def _dot_tn(a, b):
    return lax.dot_general(a, b, (((0,), (0,)), ((), ())), preferred_element_type=F32)


def _rms(x, g):
    ms = jnp.mean(x * x, axis=-1, keepdims=True)
    return x * lax.rsqrt(ms + EPS) * g


def _ffn_body(x_ref, g_ref, w1_ref, w3_ref, w2_ref, o_ref, *, ff_chunk):
    x = x_ref[...]
    h = _rms(x, g_ref[...]).astype(BF16)
    acc = None
    for f0 in range(0, D_FF, ff_chunk):
        a = _dot(h, w1_ref[:, f0:f0 + ff_chunk])
        b = _dot(h, w3_ref[:, f0:f0 + ff_chunk])
        u = (a * jax.nn.sigmoid(a) * b).astype(BF16)
        part = _dot(u, w2_ref[f0:f0 + ff_chunk, :])
        acc = part if acc is None else acc + part
    o_ref[...] = x + 0.5 * acc


def _ffn(x, g, w1, w3, w2, l, *, tm=512, ff_chunk=1408):
    n, d = x.shape
    return pl.pallas_call(
        functools.partial(_ffn_body, ff_chunk=ff_chunk),
        grid=(n // tm,),
        in_specs=[pl.BlockSpec((tm, d), lambda i: (i, 0)),
                  _layer(g, l), _layer(w1, l), _layer(w3, l), _layer(w2, l)],
        out_specs=pl.BlockSpec((tm, d), lambda i: (i, 0)),
        out_shape=jax.ShapeDtypeStruct((n, d), F32),
        compiler_params=_cparams("parallel"),
        name="ffn",
    )(x, g, w1, w3, w2)


def _inproj_body(x_ref, g_ref, w_ref, qg_ref, kg_ref, hsum_ref,
                 naq_ref, nak_ref, nav_ref, gq_ref, gk_ref, gv_ref, gr_ref, gfb_ref,
                 cq_ref, ckv_ref, kr_ref, gt_ref):
    h = _rms(x_ref[...], g_ref[...]).astype(BF16)

    def proj(c0, n):
        return _dot(h, w_ref[:, c0:c0 + n])

    hsum = hsum_ref[...]

    def head_norm(t, gain):
        sq = t * t
        hi = sq.astype(BF16)
        lo = (sq - hi.astype(F32)).astype(BF16)
        ss = _dot(hi, hsum) + _dot(lo, hsum)
        return t * lax.rsqrt(ss * (1.0 / NA_HEAD_DIM) + EPS) * gain

    naq_ref[...] = head_norm(proj(C_NA, NA_W), qg_ref[...]).astype(BF16)
    nak_ref[...] = head_norm(proj(C_NA + NA_W, NA_W), kg_ref[...]).astype(BF16)
    nav_ref[...] = proj(C_NA + 2 * NA_W, NA_W).astype(BF16)
    gq_ref[...] = proj(C_GQ, GLA_QK_W)
    gk_ref[...] = proj(C_GK, GLA_QK_W)
    gv_ref[...] = proj(C_GV, GLA_V_W)
    gr_ref[...] = proj(C_GR, GLA_V_W)
    gfb_ref[...] = proj(C_GFB, LANES)
    cq_ref[...] = proj(C_CQ, MLA_Q_RANK)
    ckv_ref[...] = proj(C_CKV, MLA_KV_RANK)
    kr_ref[...] = proj(C_KR, LANES)
    for i in range(N_BRANCH):
        c0 = C_GATE + i * D_MODEL
        gt_ref[:, i * D_MODEL:(i + 1) * D_MODEL] = jax.nn.sigmoid(proj(c0, D_MODEL)).astype(BF16)


def _inproj(x, g, w, qg, kg, hsum, l, *, tm=512):
    n, d = x.shape
    widths = [(NA_W, BF16), (NA_W, BF16), (NA_W, BF16),
              (GLA_QK_W, F32), (GLA_QK_W, F32), (GLA_V_W, F32), (GLA_V_W, F32), (LANES, F32),
              (MLA_Q_RANK, F32), (MLA_KV_RANK, F32), (LANES, F32),
              (N_BRANCH * D_MODEL, BF16)]
    return pl.pallas_call(
        _inproj_body,
        grid=(n // tm,),
        in_specs=[pl.BlockSpec((tm, d), lambda i: (i, 0)),
                  _layer(g, l), _layer(w, l), _layer(qg, l), _layer(kg, l), _resident(hsum.shape)],
        out_specs=[pl.BlockSpec((tm, c), lambda i: (i, 0)) for c, _ in widths],
        out_shape=[jax.ShapeDtypeStruct((n, c), dt) for c, dt in widths],
        compiler_params=_cparams("parallel"),
        name="inproj",
    )(x, g, w, qg, kg, hsum)


def _na_body(q_ref, k_ref, v_ref, bias_ref, o_ref, s_ref, p_ref, *, rb, rows):
    kw_tokens = NA_WIN_R * GRID_W
    pairs = NA_HEADS // 2
    unit_rows = 2 * GRID_W
    lane = lax.broadcasted_iota(jnp.int32, (1, LANES), 1)
    first_head = lane < NA_HEAD_DIM
    r_base = pl.program_id(1) * rb

    def window(i):
        r = r_base + i
        r0 = jnp.clip(r - NA_WIN_R // 2, 0, rows - NA_WIN_R)
        return r - r0, pl.ds(pl.multiple_of(r0 * GRID_W, GRID_W), kw_tokens)

    for i in range(rb):
        delta, win = window(i)
        kw = k_ref[0, win, :]
        q = q_ref[0, i * GRID_W:(i + 1) * GRID_W, :]
        for p in range(pairs):
            sl = slice(p * LANES, (p + 1) * LANES)
            qs = q[:, sl]
            zero = jnp.zeros_like(qs)
            lhs = jnp.concatenate([jnp.where(first_head, qs, zero), jnp.where(first_head, zero, qs)], axis=0)
            u = i * pairs + p
            s_ref[u * unit_rows:(u + 1) * unit_rows, :] = _dot_nt(lhs, kw[:, sl]) + bias_ref[delta, p]

    s = s_ref[...]
    p_ref[...] = jnp.exp(s - jnp.max(s, axis=-1, keepdims=True)).astype(BF16)

    ones = jnp.ones((kw_tokens, LANES), BF16)
    for i in range(rb):
        _, win = window(i)
        vw = v_ref[0, win, :]
        for p in range(pairs):
            sl = slice(p * LANES, (p + 1) * LANES)
            u = i * pairs + p
            r = _dot(p_ref[u * unit_rows:(u + 1) * unit_rows, :], jnp.concatenate([vw[:, sl], ones], axis=1))
            o = r[:, :LANES] / r[:, LANES:]
            o_ref[0, i * GRID_W:(i + 1) * GRID_W, sl] = jnp.where(first_head, o[:GRID_W], o[GRID_W:]).astype(BF16)


def _na_attention(q, k, v, bias, l, *, rb=8):
    b, s, w = q.shape
    rows = s // GRID_W
    assert rows >= NA_WIN_R and rows % rb == 0
    units = rb * (NA_HEADS // 2) * 2 * GRID_W
    return pl.pallas_call(
        functools.partial(_na_body, rb=rb, rows=rows),
        grid=(b, rows // rb),
        in_specs=[pl.BlockSpec((1, rb * GRID_W, w), lambda bi, ri: (bi, ri, 0)),
                  pl.BlockSpec((1, s, w), lambda bi, ri: (bi, 0, 0)),
                  pl.BlockSpec((1, s, w), lambda bi, ri: (bi, 0, 0)),
                  _layer(bias, l)],
        out_specs=pl.BlockSpec((1, rb * GRID_W, w), lambda bi, ri: (bi, ri, 0)),
        out_shape=jax.ShapeDtypeStruct((b, s, w), BF16),
        scratch_shapes=[pltpu.VMEM((units, NA_WIN_R * GRID_W), F32),
                        pltpu.VMEM((units, NA_WIN_R * GRID_W), BF16)],
        compiler_params=_cparams("parallel", "arbitrary"),
        name="na_attn",
    )(q, k, v, bias)


def _na_bias_table(rpb):
    nl = rpb.shape[0]
    qc = np.arange(GRID_W)
    kc = np.arange(GRID_W)
    dc = np.clip(kc[None, :] - qc[:, None] + (NA_WIN_C - 1), 0, 2 * NA_WIN_C - 2)
    c0 = np.clip(qc - NA_WIN_C // 2, 0, GRID_W - NA_WIN_C)
    ok = (kc[None, :] >= c0[:, None]) & (kc[None, :] < c0[:, None] + NA_WIN_C)
    onehot = jnp.asarray((np.arange(2 * NA_WIN_C - 1)[:, None, None] == dc[None]).astype(np.float32))
    tab = jnp.einsum("lhrc,cqk->lhrqk", rpb.astype(F32), onehot, precision=lax.Precision.HIGHEST)
    tab = jnp.where(ok[None, None, None], tab, -1e30)
    per_delta = [tab[:, :, NA_WIN_R - 1 - d:2 * NA_WIN_R - 1 - d] for d in range(NA_WIN_R)]
    tab = jnp.stack(per_delta, axis=1)
    tab = jnp.transpose(tab, (0, 1, 2, 4, 3, 5))
    return tab.reshape(nl, NA_WIN_R, NA_HEADS // 2, 2 * GRID_W, NA_WIN_R * GRID_W)


def _gla_body(gq_ref, gk_ref, gv_ref, gfb_ref, gr_ref, up_ref, gbias_ref, onorm_ref, tri_ref,
              y_ref, ob_ref, st_ref, upd_ref, sin_ref, a_ref, o_ref, *, nb, tb):
    t = pl.program_id(1)
    nc = tb // GLA_CHUNK
    lane = lax.broadcasted_iota(jnp.int32, (1, LANES), 1)
    first_head = lane < GLA_DK
    ci = lax.broadcasted_iota(jnp.int32, (GLA_CHUNK, GLA_CHUNK), 0)
    cj = lax.broadcasted_iota(jnp.int32, (GLA_CHUNK, GLA_CHUNK), 1)

    @pl.when((t == 0) | (t == nb))
    def _():
        st_ref[...] = jnp.zeros_like(st_ref)

    def scan_block(bwd):
        blk = (nb - 1 - t) if bwd else (t - nb)
        row0 = pl.multiple_of(blk * tb, tb)
        gsl = slice(GLA_QK_W, 2 * GLA_QK_W) if bwd else slice(0, GLA_QK_W)
        x = _dot(gfb_ref[0].astype(BF16), up_ref[:, gsl]) + gbias_ref[:, gsl]
        g = (jnp.minimum(x, 0.0) - jnp.log1p(jnp.exp(-jnp.abs(x)))) * (1.0 / GLA_GATE_TAU)
        g_hi = g.astype(BF16)
        g_lo = (g - g_hi.astype(F32)).astype(BF16)
        tri = tri_ref[1 if bwd else 0]
        cum = _dot(tri, g_hi) + _dot(tri, g_lo)
        k = gk_ref[0]
        qe = (gq_ref[0] * (GLA_DK ** -0.5) * jnp.exp(cum)).astype(BF16)
        ke = (k * jnp.exp(-cum)).astype(BF16)
        v = gv_ref[0].astype(BF16)
        keep = (cj > ci) if bwd else (cj <= ci)
        order = list(range(nc - 1, -1, -1) if bwd else range(nc))
        zero = jnp.zeros((GLA_CHUNK, LANES), BF16)

        def rows_of(c):
            return slice(c * GLA_CHUNK, (c + 1) * GLA_CHUNK)

        def lanes_of(h):
            return slice((h // 2) * LANES, (h // 2 + 1) * LANES), slice(h * GLA_DV, (h + 1) * GLA_DV)

        def q_masked(c, h):
            ps, _ = lanes_of(h)
            qs = qe[rows_of(c), ps]
            return jnp.where(first_head, qs, zero) if h % 2 == 0 else jnp.where(first_head, zero, qs)

        decays = {}
        for c in order:
            rs = rows_of(c)
            edge = c * GLA_CHUNK if bwd else (c + 1) * GLA_CHUNK - 1
            cum_end = cum[edge:edge + 1, :]
            k_end = (k[rs] * jnp.exp(cum_end - cum[rs])).astype(BF16)
            decays[c] = jnp.exp(cum_end)
            for h in range(GLA_HEADS):
                ps, vs = lanes_of(h)
                a_ref[c, h] = jnp.where(keep, _dot_nt(q_masked(c, h), ke[rs, ps]), 0.0).astype(BF16)
                upd_ref[c, h] = _dot_tn(v[rs, vs], k_end[:, ps])
        for h in range(GLA_HEADS):
            ps, _ = lanes_of(h)
            st = st_ref[h]
            for c in order:
                sin_ref[c, h] = st.astype(BF16)
                st = decays[c][:, ps] * st + upd_ref[c, h]
            st_ref[h] = st
        for c in order:
            for h in range(GLA_HEADS):
                _, vs = lanes_of(h)
                o_ref[rows_of(c), vs] = _dot(a_ref[c, h], v[rows_of(c), vs]) + _dot_nt(q_masked(c, h), sin_ref[c, h])
        return row0

    @pl.when(t < nb)
    def _():
        row0 = scan_block(True)
        ob_ref[pl.ds(row0, tb), :] = o_ref[...]

    @pl.when(t >= nb)
    def _():
        row0 = scan_block(False)
        tot = o_ref[...] + ob_ref[pl.ds(row0, tb), :]
        gr = gr_ref[0]
        for h in range(GLA_HEADS):
            vs = slice(h * GLA_DV, (h + 1) * GLA_DV)
            y = _rms(tot[:, vs], onorm_ref[:, vs])
            gate = gr[:, vs]
            y_ref[0, :, vs] = (y * (gate * jax.nn.sigmoid(gate))).astype(BF16)


def _gla(gq, gk, gv, gfb, gr, up, gbias, onorm, tri, l, *, tb=512):
    b, s, _ = gq.shape
    nb = s // tb
    nc = tb // GLA_CHUNK
    assert tri.shape == (2, tb, tb)

    def scan_idx(bi, t):
        return (bi, jnp.where(t < nb, nb - 1 - t, t - nb), 0)

    def fwd_idx(bi, t):
        return (bi, jnp.maximum(t - nb, 0), 0)

    return pl.pallas_call(
        functools.partial(_gla_body, nb=nb, tb=tb),
        grid=(b, 2 * nb),
        in_specs=[pl.BlockSpec((1, tb, GLA_QK_W), scan_idx),
                  pl.BlockSpec((1, tb, GLA_QK_W), scan_idx),
                  pl.BlockSpec((1, tb, GLA_V_W), scan_idx),
                  pl.BlockSpec((1, tb, LANES), scan_idx),
                  pl.BlockSpec((1, tb, GLA_V_W), fwd_idx),
                  _layer(up, l), _layer(gbias, l), _layer(onorm, l), _resident(tri.shape)],
        out_specs=pl.BlockSpec((1, tb, GLA_V_W), fwd_idx),
        out_shape=jax.ShapeDtypeStruct((b, s, GLA_V_W), BF16),
        scratch_shapes=[pltpu.VMEM((s, GLA_V_W), F32),
                        pltpu.VMEM((GLA_HEADS, GLA_DV, LANES), F32),
                        pltpu.VMEM((nc, GLA_HEADS, GLA_DV, LANES), F32),
                        pltpu.VMEM((nc, GLA_HEADS, GLA_DV, LANES), BF16),
                        pltpu.VMEM((nc, GLA_HEADS, GLA_CHUNK, GLA_CHUNK), BF16),
                        pltpu.VMEM((tb, GLA_V_W), F32)],
        compiler_params=_cparams("arbitrary", "arbitrary"),
        name="gla",
    )(gq, gk, gv, gfb, gr, up, gbias, onorm, tri)


def _gla_tri(tb):
    i = np.arange(tb)
    same = (i[:, None] // GLA_CHUNK) == (i[None, :] // GLA_CHUNK)
    lower = same & (i[None, :] <= i[:, None])
    upper = same & (i[None, :] >= i[:, None])
    return jnp.asarray(np.stack([lower, upper]).astype(np.float32), BF16)


def _mla_prep_body(cq_ref, ckv_ref, kr_ref, cqg_ref, ckvg_ref, wuq_ref, wukv_ref,
                   qg_ref, kgn_ref, kgr_ref, cos_ref, sa_ref, sb_ref,
                   q_ref, k_ref, v_ref):
    cos, sa, sb = cos_ref[...], sa_ref[...], sb_ref[...]

    def rope(slab):
        return slab * cos + pltpu.roll(slab, LANES - MLA_ROPE // 2, 1) * sa + pltpu.roll(slab, MLA_ROPE // 2, 1) * sb

    inv_d = 1.0 / MLA_QK_HEAD
    q = _dot(_rms(cq_ref[...], cqg_ref[...]).astype(BF16), wuq_ref[...])
    for h in range(MLA_HEADS):
        qh = q[:, h * MLA_SLAB:(h + 1) * MLA_SLAB]
        ss = jnp.sum(qh * qh, axis=-1, keepdims=True) * inv_d
        qn = qh * lax.rsqrt(ss + EPS) * qg_ref[...]
        q_ref[:, h * MLA_SLAB:h * MLA_SLAB + LANES] = qn[:, :LANES].astype(BF16)
        q_ref[:, h * MLA_SLAB + LANES:(h + 1) * MLA_SLAB] = rope(qn[:, LANES:]).astype(BF16)

    kv = _dot(_rms(ckv_ref[...], ckvg_ref[...]).astype(BF16), wukv_ref[...])
    kr = kr_ref[...]
    ss_r = jnp.sum(kr * kr, axis=-1, keepdims=True)
    kr_rot = rope(kr * kgr_ref[...])
    for h in range(MLA_HEADS):
        kn = kv[:, h * MLA_NOPE:(h + 1) * MLA_NOPE]
        ss = (jnp.sum(kn * kn, axis=-1, keepdims=True) + ss_r) * inv_d
        inv = lax.rsqrt(ss + EPS)
        k_ref[:, h * MLA_SLAB:h * MLA_SLAB + LANES] = (kn * inv * kgn_ref[...]).astype(BF16)
        k_ref[:, h * MLA_SLAB + LANES:(h + 1) * MLA_SLAB] = (kr_rot * inv).astype(BF16)
    v_ref[...] = kv[:, MLA_HEADS * MLA_NOPE:].astype(BF16)


def _mla_prep(cq, ckv, kr, cqg, ckvg, wuq, wukv, qg, kgn, kgr, cos, sa, sb, l, *, seq, tm=512):
    n = cq.shape[0]
    pos_blocks = seq // tm
    tok = lambda c: pl.BlockSpec((tm, c), lambda i: (i, 0))
    pos = pl.BlockSpec((tm, LANES), lambda i: (i % pos_blocks, 0))
    return pl.pallas_call(
        _mla_prep_body,
        grid=(n // tm,),
        in_specs=[tok(MLA_Q_RANK), tok(MLA_KV_RANK), tok(LANES),
                  _layer(cqg, l), _layer(ckvg, l), _layer(wuq, l), _layer(wukv, l),
                  _layer(qg, l), _layer(kgn, l), _layer(kgr, l), pos, pos, pos],
        out_specs=[tok(MLA_HEADS * MLA_SLAB), tok(MLA_HEADS * MLA_SLAB), tok(MLA_V_W)],
        out_shape=[jax.ShapeDtypeStruct((n, MLA_HEADS * MLA_SLAB), BF16),
                   jax.ShapeDtypeStruct((n, MLA_HEADS * MLA_SLAB), BF16),
                   jax.ShapeDtypeStruct((n, MLA_V_W), BF16)],
        compiler_params=_cparams("parallel"),
        name="mla_prep",
    )(cq, ckv, kr, cqg, ckvg, wuq, wukv, qg, kgn, kgr, cos, sa, sb)


def _mla_attn_body(q_ref, k_ref, v_ref, o_ref, *, tk):
    q = q_ref[0]
    tq = q.shape[0]
    s_len = k_ref.shape[1]
    ones = jnp.ones((tk, LANES), BF16)
    m = jnp.full((tq, 1), -jnp.inf, F32)
    acc = jnp.zeros((tq, MLA_V + LANES), F32)
    for j in range(s_len // tk):
        kb = k_ref[0, j * tk:(j + 1) * tk, :]
        vb = v_ref[0, j * tk:(j + 1) * tk, :]
        s = _dot_nt(q, kb)
        m_new = jnp.maximum(m, jnp.max(s, axis=-1, keepdims=True))
        p = jnp.exp2(s - m_new).astype(BF16)
        acc = jnp.exp2(m - m_new) * acc + _dot(p, jnp.concatenate([vb, ones], axis=1))
        m = m_new
    o_ref[0] = (acc[:, :MLA_V] / acc[:, MLA_V:]).astype(BF16)


def _mla_attention(q, k, v, *, tq=512, tk=512):
    b, s, _ = q.shape
    return pl.pallas_call(
        functools.partial(_mla_attn_body, tk=tk),
        grid=(b, MLA_HEADS, s // tq),
        in_specs=[pl.BlockSpec((1, tq, MLA_SLAB), lambda bi, h, qi: (bi, qi, h)),
                  pl.BlockSpec((1, s, MLA_SLAB), lambda bi, h, qi: (bi, 0, h)),
                  pl.BlockSpec((1, s, MLA_V), lambda bi, h, qi: (bi, 0, h))],
        out_specs=pl.BlockSpec((1, tq, MLA_V), lambda bi, h, qi: (bi, qi, h)),
        out_shape=jax.ShapeDtypeStruct((b, s, MLA_V_W), BF16),
        compiler_params=_cparams("parallel", "parallel", "arbitrary"),
        name="mla_attn",
    )(q, k, v)


def _merge_body(x_ref, yna_ref, ygla_ref, ymla_ref, gt_ref, wbr_ref, wout_ref, o_ref):
    mixed = None
    for i, y_ref in enumerate((yna_ref, ygla_ref, ymla_ref)):
        br = _dot(y_ref[...], wbr_ref[i])
        term = gt_ref[:, i * D_MODEL:(i + 1) * D_MODEL].astype(F32) * br
        mixed = term if mixed is None else mixed + term
    o_ref[...] = x_ref[...] + _dot(mixed.astype(BF16), wout_ref[...])


def _merge(x, yna, ygla, ymla, gt, wbr, wout, l, *, tm=512):
    n, d = x.shape
    tok = lambda c: pl.BlockSpec((tm, c), lambda i: (i, 0))
    return pl.pallas_call(
        _merge_body,
        grid=(n // tm,),
        in_specs=[tok(d), tok(NA_W), tok(GLA_V_W), tok(MLA_V_W), tok(N_BRANCH * d),
                  _layer(wbr, l), _layer(wout, l)],
        out_specs=tok(d),
        out_shape=jax.ShapeDtypeStruct((n, d), F32),
        compiler_params=_cparams("parallel"),
        name="merge",
    )(x, yna, ygla, ymla, gt, wbr, wout)


def _rope_tables(s):
    half = MLA_ROPE // 2
    inv = ROPE_THETA ** (-jnp.arange(half, dtype=F32) / half)
    ang = jnp.arange(s, dtype=F32)[:, None] * inv[None, :]
    cos, sin = jnp.cos(ang), jnp.sin(ang)
    z = jnp.zeros((s, half), F32)
    pad = jnp.zeros((s, LANES - MLA_ROPE), F32)
    return (jnp.concatenate([cos, cos, pad], axis=1),
            jnp.concatenate([-sin, z, pad], axis=1),
            jnp.concatenate([z, sin, pad], axis=1))


def _pack_w_in(w_in):
    nl, d, _ = w_in.shape
    gfb_end = 3 * NA_W + 2 * GLA_QK_W + 2 * GLA_V_W + 2 * GLA_GATE_RANK
    kr_end = gfb_end + MLA_Q_RANK + MLA_KV_RANK + MLA_ROPE
    z = lambda c: jnp.zeros((nl, d, c), BF16)
    w = w_in.astype(BF16)
    return jnp.concatenate([w[:, :, :gfb_end], z(LANES - 2 * GLA_GATE_RANK),
                            w[:, :, gfb_end:kr_end], z(LANES - MLA_ROPE),
                            w[:, :, kr_end:]], axis=2)


def kernel(x, ffn1_norm, ffn1_w1, ffn1_w3, ffn1_w2, mix_norm, w_in, na_q_norm, na_k_norm, na_rpb, gla_gf_up, gla_gf_bias, gla_gb_up, gla_gb_bias, gla_out_norm, mla_cq_norm, mla_ckv_norm, mla_w_uq, mla_w_ukv, mla_q_norm, mla_k_norm, w_br_na, w_br_gla, w_br_mla, w_out, ffn2_norm, ffn2_w1, ffn2_w3, ffn2_w2):
    b, s, d = x.shape
    nl = w_in.shape[0]
    n = b * s
    gla_tb = 512

    row = lambda a: a[:, None, :].astype(F32)
    f1 = (row(ffn1_norm), ffn1_w1.astype(BF16), ffn1_w3.astype(BF16), ffn1_w2.astype(BF16))
    f2 = (row(ffn2_norm), ffn2_w1.astype(BF16), ffn2_w3.astype(BF16), ffn2_w2.astype(BF16))
    mixg = row(mix_norm)
    w_in_p = _pack_w_in(w_in)
    na_qg = row(jnp.tile(na_q_norm, (1, NA_HEADS)) * (NA_HEAD_DIM ** -0.5))
    na_kg = row(jnp.tile(na_k_norm, (1, NA_HEADS)))
    head_of = np.arange(NA_W) // NA_HEAD_DIM
    hsum = jnp.asarray((head_of[:, None] == head_of[None, :]).astype(np.float32), BF16)
    na_bias = _na_bias_table(na_rpb)

    up = jnp.zeros((nl, LANES, 2 * GLA_QK_W), F32)
    up = up.at[:, :GLA_GATE_RANK, :GLA_QK_W].set(gla_gf_up)
    up = up.at[:, GLA_GATE_RANK:2 * GLA_GATE_RANK, GLA_QK_W:].set(gla_gb_up).astype(BF16)
    gbias = row(jnp.concatenate([gla_gf_bias, gla_gb_bias], axis=1))
    onorm = row(jnp.tile(gla_out_norm, (1, GLA_HEADS)))
    tri = _gla_tri(gla_tb)

    cqg, ckvg = row(mla_cq_norm), row(mla_ckv_norm)
    wuq = mla_w_uq.reshape(nl, MLA_Q_RANK, MLA_HEADS, MLA_QK_HEAD)
    wuq = jnp.pad(wuq, ((0, 0), (0, 0), (0, 0), (0, MLA_SLAB - MLA_QK_HEAD)))
    wuq = wuq.reshape(nl, MLA_Q_RANK, MLA_HEADS * MLA_SLAB).astype(BF16)
    wukv = mla_w_ukv.reshape(nl, MLA_KV_RANK, MLA_HEADS, 2, MLA_NOPE)
    wukv = jnp.swapaxes(wukv, 2, 3).reshape(nl, MLA_KV_RANK, 2 * MLA_HEADS * MLA_NOPE).astype(BF16)
    q_fold = (MLA_QK_HEAD ** -0.5) * math.log2(math.e)
    mqg = row(jnp.pad(mla_q_norm, ((0, 0), (0, MLA_SLAB - MLA_QK_HEAD))) * q_fold)
    mkgn = row(mla_k_norm[:, :MLA_NOPE])
    mkgr = row(jnp.pad(mla_k_norm[:, MLA_NOPE:], ((0, 0), (0, LANES - MLA_ROPE))))
    cos, sa, sb = _rope_tables(s)

    wbr = jnp.stack([w_br_na, w_br_gla, w_br_mla], axis=1).astype(BF16)
    wout = w_out.astype(BF16)

    xf = x.reshape(n, d).astype(F32)
    seq3 = lambda a: a.reshape(b, s, a.shape[-1])
    for l in range(nl):
        xf = _ffn(xf, *f1, l)
        (naq, nak, nav, gq, gk, gv, gr, gfb, cq, ckv, kr, gt) = _inproj(xf, mixg, w_in_p, na_qg, na_kg, hsum, l)
        y_na = _na_attention(seq3(naq), seq3(nak), seq3(nav), na_bias, l)
        y_gla = _gla(seq3(gq), seq3(gk), seq3(gv), seq3(gfb), seq3(gr), up, gbias, onorm, tri, l, tb=gla_tb)
        mq, mk, mv = _mla_prep(cq, ckv, kr, cqg, ckvg, wuq, wukv, mqg, mkgn, mkgr, cos, sa, sb, l, seq=s)
        y_mla = _mla_attention(seq3(mq), seq3(mk), seq3(mv))
        xf = _merge(xf, y_na.reshape(n, NA_W), y_gla.reshape(n, GLA_V_W), y_mla.reshape(n, MLA_V_W),
                    gt, wbr, wout, l)
        xf = _ffn(xf, *f2, l)
    return xf.reshape(b, s, d).astype(x.dtype)
```

```python
import functools
import math

import numpy as np
import jax
import jax.numpy as jnp
from jax import lax
from jax.experimental import pallas as pl
from jax.experimental.pallas import tpu as pltpu

F32 = jnp.float32
BF16 = jnp.bfloat16

EPS = 1e-6
D_MODEL = 1024
D_FF = 2816
GRID_W = 64
NA_HEADS = 8
NA_HEAD_DIM = 64
NA_WIN_R = 8
NA_WIN_C = 16
NA_W = NA_HEADS * NA_HEAD_DIM
GLA_HEADS = 4
GLA_DK = 64
GLA_DV = 128
GLA_GATE_RANK = 16
GLA_GATE_TAU = 16.0
GLA_CHUNK = 64
GLA_QK_W = GLA_HEADS * GLA_DK
GLA_V_W = GLA_HEADS * GLA_DV
MLA_HEADS = 4
MLA_Q_RANK = 256
MLA_KV_RANK = 256
MLA_NOPE = 128
MLA_ROPE = 64
MLA_V = 128
MLA_QK_HEAD = MLA_NOPE + MLA_ROPE
MLA_SLAB = 256
MLA_V_W = MLA_HEADS * MLA_V
ROPE_THETA = 10000.0
N_BRANCH = 3

LANES = 128
C_NA = 0
C_GQ = 3 * NA_W
C_GK = C_GQ + GLA_QK_W
C_GV = C_GK + GLA_QK_W
C_GR = C_GV + GLA_V_W
C_GFB = C_GR + GLA_V_W
C_CQ = C_GFB + LANES
C_CKV = C_CQ + MLA_Q_RANK
C_KR = C_CKV + MLA_KV_RANK
C_GATE = C_KR + LANES
D_IN_PACKED = C_GATE + N_BRANCH * D_MODEL

VMEM_LIMIT = 56 * 1024 * 1024


def _cparams(*sem):
    return pltpu.CompilerParams(dimension_semantics=sem, vmem_limit_bytes=VMEM_LIMIT)


def _resident(shape):
    nd = len(shape)
    return pl.BlockSpec(shape, lambda *_: (0,) * nd, pipeline_mode=pl.Buffered(1))


def _layer(arr, l):
    nd = arr.ndim - 1
    return pl.BlockSpec((None,) + tuple(arr.shape[1:]), lambda *_: (l,) + (0,) * nd,
                        pipeline_mode=pl.Buffered(1))


def _dot(a, b):
    return jnp.dot(a, b, preferred_element_type=F32)


def _dot_nt(a, b):
    return lax.dot_general(a, b, (((1,), (1,)), ((), ())), preferred_element_type=F32)


def _dot_tn(a, b):
    return lax.dot_general(a, b, (((0,), (0,)), ((), ())), preferred_element_type=F32)


def _rms(x, g):
    ms = jnp.mean(x * x, axis=-1, keepdims=True)
    return x * lax.rsqrt(ms + EPS) * g


def _ffn_body(x_ref, g_ref, w1_ref, w3_ref, w2_ref, o_ref, *, ff_chunk):
    x = x_ref[...]
    h = _rms(x, g_ref[...]).astype(BF16)
    acc = None
    for f0 in range(0, D_FF, ff_chunk):
        a = _dot(h, w1_ref[:, f0:f0 + ff_chunk])
        b = _dot(h, w3_ref[:, f0:f0 + ff_chunk])
        u = (a * jax.nn.sigmoid(a) * b).astype(BF16)
        part = _dot(u, w2_ref[f0:f0 + ff_chunk, :])
        acc = part if acc is None else acc + part
    o_ref[...] = x + 0.5 * acc


def _ffn(x, g, w1, w3, w2, l, *, tm=512, ff_chunk=1408):
    n, d = x.shape
    return pl.pallas_call(
        functools.partial(_ffn_body, ff_chunk=ff_chunk),
        grid=(n // tm,),
        in_specs=[pl.BlockSpec((tm, d), lambda i: (i, 0)),
                  _layer(g, l), _layer(w1, l), _layer(w3, l), _layer(w2, l)],
        out_specs=pl.BlockSpec((tm, d), lambda i: (i, 0)),
        out_shape=jax.ShapeDtypeStruct((n, d), F32),
        compiler_params=_cparams("parallel"),
        name="ffn",
    )(x, g, w1, w3, w2)


def _mla_qkv(cq, ckv, kr, cqg, ckvg, wuq, wukv, qg, kgn, kgr, cos, sa, sb, q_ref, k_ref, v_ref):
    def rope(slab):
        return slab * cos + pltpu.roll(slab, LANES - MLA_ROPE // 2, 1) * sa + pltpu.roll(slab, MLA_ROPE // 2, 1) * sb

    inv_d = 1.0 / MLA_QK_HEAD
    q = _dot(_rms(cq, cqg).astype(BF16), wuq)
    for h in range(MLA_HEADS):
        qh = q[:, h * MLA_SLAB:(h + 1) * MLA_SLAB]
        ss = jnp.sum(qh * qh, axis=-1, keepdims=True) * inv_d
        qn = qh * lax.rsqrt(ss + EPS) * qg
        q_ref[:, h * MLA_SLAB:h * MLA_SLAB + LANES] = qn[:, :LANES].astype(BF16)
        q_ref[:, h * MLA_SLAB + LANES:(h + 1) * MLA_SLAB] = rope(qn[:, LANES:]).astype(BF16)

    kv = _dot(_rms(ckv, ckvg).astype(BF16), wukv)
    ss_r = jnp.sum(kr * kr, axis=-1, keepdims=True)
    kr_rot = rope(kr * kgr)
    for h in range(MLA_HEADS):
        kn = kv[:, h * MLA_NOPE:(h + 1) * MLA_NOPE]
        ss = (jnp.sum(kn * kn, axis=-1, keepdims=True) + ss_r) * inv_d
        inv = lax.rsqrt(ss + EPS)
        k_ref[:, h * MLA_SLAB:h * MLA_SLAB + LANES] = (kn * inv * kgn).astype(BF16)
        k_ref[:, h * MLA_SLAB + LANES:(h + 1) * MLA_SLAB] = (kr_rot * inv).astype(BF16)
    v_ref[...] = kv[:, MLA_HEADS * MLA_NOPE:].astype(BF16)


def _inproj_body(x_ref, g_ref, w_ref, qg_ref, kg_ref, hsum_ref,
                 cqg_ref, ckvg_ref, wuq_ref, wukv_ref, mqg_ref, mkgn_ref, mkgr_ref, cos_ref, sa_ref, sb_ref,
                 naq_ref, nak_ref, nav_ref, gq_ref, gk_ref, gv_ref, gr_ref, gfb_ref,
                 mq_ref, mk_ref, mv_ref, gt_ref):
    h = _rms(x_ref[...], g_ref[...]).astype(BF16)

    def proj(c0, n):
        return _dot(h, w_ref[:, c0:c0 + n])

    hsum = hsum_ref[...]

    def head_norm(t, gain):
        ss = _dot((t * t).astype(BF16), hsum)
        return t * lax.rsqrt(ss * (1.0 / NA_HEAD_DIM) + EPS) * gain

    naq_ref[...] = head_norm(proj(C_NA, NA_W), qg_ref[...]).astype(BF16)
    nak_ref[...] = head_norm(proj(C_NA + NA_W, NA_W), kg_ref[...]).astype(BF16)
    nav_ref[...] = proj(C_NA + 2 * NA_W, NA_W).astype(BF16)
    gq_ref[...] = proj(C_GQ, GLA_QK_W)
    gk_ref[...] = proj(C_GK, GLA_QK_W)
    gv_ref[...] = proj(C_GV, GLA_V_W)
    gr_ref[...] = proj(C_GR, GLA_V_W)
    gfb_ref[...] = proj(C_GFB, LANES)
    _mla_qkv(proj(C_CQ, MLA_Q_RANK), proj(C_CKV, MLA_KV_RANK), proj(C_KR, LANES),
             cqg_ref[...], ckvg_ref[...], wuq_ref[...], wukv_ref[...],
             mqg_ref[...], mkgn_ref[...], mkgr_ref[...], cos_ref[...], sa_ref[...], sb_ref[...],
             mq_ref, mk_ref, mv_ref)
    for i in range(N_BRANCH):
        c0 = C_GATE + i * D_MODEL
        gt_ref[:, i * D_MODEL:(i + 1) * D_MODEL] = jax.nn.sigmoid(proj(c0, D_MODEL)).astype(BF16)


def _inproj(x, g, w, qg, kg, hsum, cqg, ckvg, wuq, wukv, mqg, mkgn, mkgr, cos, sa, sb, l, *, seq, tm=512):
    n, d = x.shape
    widths = [(NA_W, BF16), (NA_W, BF16), (NA_W, BF16),
              (GLA_QK_W, F32), (GLA_QK_W, F32), (GLA_V_W, F32), (GLA_V_W, F32), (LANES, F32),
              (MLA_HEADS * MLA_SLAB, BF16), (MLA_HEADS * MLA_SLAB, BF16), (MLA_V_W, BF16),
              (N_BRANCH * D_MODEL, BF16)]
    pos_blocks = seq // tm
    pos = pl.BlockSpec((tm, LANES), lambda i: (i % pos_blocks, 0))
    return pl.pallas_call(
        _inproj_body,
        grid=(n // tm,),
        in_specs=[pl.BlockSpec((tm, d), lambda i: (i, 0)),
                  _layer(g, l), _layer(w, l), _layer(qg, l), _layer(kg, l), _resident(hsum.shape),
                  _layer(cqg, l), _layer(ckvg, l), _layer(wuq, l), _layer(wukv, l),
                  _layer(mqg, l), _layer(mkgn, l), _layer(mkgr, l), pos, pos, pos],
        out_specs=[pl.BlockSpec((tm, c), lambda i: (i, 0)) for c, _ in widths],
        out_shape=[jax.ShapeDtypeStruct((n, c), dt) for c, dt in widths],
        compiler_params=_cparams("parallel"),
        name="inproj",
    )(x, g, w, qg, kg, hsum, cqg, ckvg, wuq, wukv, mqg, mkgn, mkgr, cos, sa, sb)


def _na_body(q_ref, k_ref, v_ref, tab_ref, o_ref, s_ref, p_ref, bias_ref, *, rb, rows):
    kw_tokens = NA_WIN_R * GRID_W
    pairs = NA_HEADS // 2
    unit_rows = 2 * GRID_W
    lane = lax.broadcasted_iota(jnp.int32, (1, LANES), 1)
    first_head = lane < NA_HEAD_DIM
    r_base = pl.program_id(1) * rb

    @pl.when((pl.program_id(0) == 0) & (pl.program_id(1) == 0))
    def _():
        for d in range(NA_WIN_R):
            off = (NA_WIN_R - 1 - d) * GRID_W
            for p in range(pairs):
                bias_ref[d, p] = tab_ref[p, :, off:off + kw_tokens]

    def window(i):
        r = r_base + i
        r0 = jnp.clip(r - NA_WIN_R // 2, 0, rows - NA_WIN_R)
        return r - r0, pl.ds(pl.multiple_of(r0 * GRID_W, GRID_W), kw_tokens)

    for i in range(rb):
        delta, win = window(i)
        kw = k_ref[0, win, :]
        q = q_ref[0, i * GRID_W:(i + 1) * GRID_W, :]
        for p in range(pairs):
            sl = slice(p * LANES, (p + 1) * LANES)
            qs = q[:, sl]
            zero = jnp.zeros_like(qs)
            lhs = jnp.concatenate([jnp.where(first_head, qs, zero), jnp.where(first_head, zero, qs)], axis=0)
            u = i * pairs + p
            s_ref[u * unit_rows:(u + 1) * unit_rows, :] = _dot_nt(lhs, kw[:, sl]) + bias_ref[delta, p]

    s = s_ref[...]
    p_ref[...] = jnp.exp(s - jnp.max(s, axis=-1, keepdims=True)).astype(BF16)

    ones = jnp.ones((kw_tokens, LANES), BF16)
    for i in range(rb):
        _, win = window(i)
        vw = v_ref[0, win, :]
        for p in range(pairs):
            sl = slice(p * LANES, (p + 1) * LANES)
            u = i * pairs + p
            r = _dot(p_ref[u * unit_rows:(u + 1) * unit_rows, :], jnp.concatenate([vw[:, sl], ones], axis=1))
            o = r[:, :LANES] / r[:, LANES:]
            o_ref[0, i * GRID_W:(i + 1) * GRID_W, sl] = jnp.where(first_head, o[:GRID_W], o[GRID_W:]).astype(BF16)


def _na_attention(q, k, v, bias, l, *, rb=8):
    b, s, w = q.shape
    rows = s // GRID_W
    assert rows >= NA_WIN_R and rows % rb == 0
    units = rb * (NA_HEADS // 2) * 2 * GRID_W
    return pl.pallas_call(
        functools.partial(_na_body, rb=rb, rows=rows),
        grid=(b, rows // rb),
        in_specs=[pl.BlockSpec((1, rb * GRID_W, w), lambda bi, ri: (bi, ri, 0)),
                  pl.BlockSpec((1, s, w), lambda bi, ri: (bi, 0, 0)),
                  pl.BlockSpec((1, s, w), lambda bi, ri: (bi, 0, 0)),
                  _layer(bias, l)],
        out_specs=pl.BlockSpec((1, rb * GRID_W, w), lambda bi, ri: (bi, ri, 0)),
        out_shape=jax.ShapeDtypeStruct((b, s, w), BF16),
        scratch_shapes=[pltpu.VMEM((units, NA_WIN_R * GRID_W), F32),
                        pltpu.VMEM((units, NA_WIN_R * GRID_W), BF16),
                        pltpu.VMEM((NA_WIN_R, NA_HEADS // 2, 2 * GRID_W, NA_WIN_R * GRID_W), F32)],
        compiler_params=_cparams("arbitrary", "arbitrary"),
        name="na_attn",
    )(q, k, v, bias)


def _na_bias_table(rpb):
    nl = rpb.shape[0]
    rel_rows = 2 * NA_WIN_R - 1
    qc = np.arange(GRID_W)
    kc = np.arange(GRID_W)
    dc = np.clip(kc[None, :] - qc[:, None] + (NA_WIN_C - 1), 0, 2 * NA_WIN_C - 2)
    c0 = np.clip(qc - NA_WIN_C // 2, 0, GRID_W - NA_WIN_C)
    ok = (kc[None, :] >= c0[:, None]) & (kc[None, :] < c0[:, None] + NA_WIN_C)
    onehot = jnp.asarray((np.arange(2 * NA_WIN_C - 1)[:, None, None] == dc[None]).astype(np.float32))
    tab = jnp.einsum("lhrc,cqk->lhqrk", rpb.astype(F32), onehot, precision=lax.Precision.HIGHEST)
    tab = jnp.where(ok[None, None, :, None, :], tab, -1e30)
    tab = tab.reshape(nl, NA_HEADS // 2, 2 * GRID_W, rel_rows * GRID_W)
    return jnp.pad(tab, ((0, 0), (0, 0), (0, 0), (0, GRID_W)))


def _gla_body(gq_ref, gk_ref, gv_ref, gfb_ref, gr_ref, up_ref, gbias_ref, onorm_ref, tri_ref,
              y_ref, ob_ref, st_ref, upd_ref, sin_ref, a_ref, o_ref, *, nb, tb):
    t = pl.program_id(1)
    nc = tb // GLA_CHUNK
    pairs = GLA_HEADS // 2
    pair_dv = 2 * GLA_DV
    dk_first = lax.broadcasted_iota(jnp.int32, (1, LANES), 1) < GLA_DK
    dv_first = lax.broadcasted_iota(jnp.int32, (1, pair_dv), 1) < GLA_DV
    ci = lax.broadcasted_iota(jnp.int32, (GLA_CHUNK, LANES), 0)
    cj = lax.broadcasted_iota(jnp.int32, (GLA_CHUNK, LANES), 1) % GLA_CHUNK
    same_head = (lax.broadcasted_iota(jnp.int32, (pair_dv, LANES), 0) < GLA_DV) == dk_first

    @pl.when((t == 0) | (t == nb))
    def _():
        st_ref[...] = jnp.zeros_like(st_ref)

    def scan_block(bwd):
        blk = (nb - 1 - t) if bwd else (t - nb)
        row0 = pl.multiple_of(blk * tb, tb)
        gsl = slice(GLA_QK_W, 2 * GLA_QK_W) if bwd else slice(0, GLA_QK_W)
        x = _dot(gfb_ref[0].astype(BF16), up_ref[:, gsl]) + gbias_ref[:, gsl]
        g = (jnp.minimum(x, 0.0) - jnp.log(1.0 + jnp.exp(-jnp.abs(x)))) * (1.0 / GLA_GATE_TAU)
        g_hi = g.astype(BF16)
        g_lo = (g - g_hi.astype(F32)).astype(BF16)
        tri = tri_ref[1 if bwd else 0]
        cum = _dot(tri, g_hi) + _dot(tri, g_lo)
        k = gk_ref[0]
        qe = (gq_ref[0] * (GLA_DK ** -0.5) * jnp.exp(cum)).astype(BF16)
        ke = (k * jnp.exp(-cum)).astype(BF16)
        v = gv_ref[0].astype(BF16)
        keep = (cj > ci) if bwd else (cj <= ci)
        order = list(range(nc - 1, -1, -1) if bwd else range(nc))

        def rows_of(c):
            return slice(c * GLA_CHUNK, (c + 1) * GLA_CHUNK)

        def lanes_of(p):
            return slice(p * LANES, (p + 1) * LANES), slice(p * pair_dv, (p + 1) * pair_dv)

        def block_diag(x, first):
            zero = jnp.zeros_like(x)
            return jnp.concatenate([jnp.where(first, x, zero), jnp.where(first, zero, x)], axis=0)

        decays = {}
        for c in order:
            rs = rows_of(c)
            edge = c * GLA_CHUNK if bwd else (c + 1) * GLA_CHUNK - 1
            cum_end = cum[edge:edge + 1, :]
            k_end = (k[rs] * jnp.exp(cum_end - cum[rs])).astype(BF16)
            decays[c] = jnp.exp(cum_end)
            for p in range(pairs):
                ps, vs = lanes_of(p)
                scores = _dot_nt(qe[rs, ps], block_diag(ke[rs, ps], dk_first))
                a_ref[c, p] = jnp.where(keep, scores, 0.0).astype(BF16)
                inc = _dot_tn(v[rs, vs], k_end[:, ps])
                upd_ref[c, p] = jnp.where(same_head, inc, 0.0)
        for p in range(pairs):
            ps, _ = lanes_of(p)
            st = st_ref[p]
            for c in order:
                sin_ref[c, p] = st.astype(BF16)
                st = decays[c][:, ps] * st + upd_ref[c, p]
            st_ref[p] = st
        for c in order:
            rs = rows_of(c)
            for p in range(pairs):
                ps, vs = lanes_of(p)
                o_ref[rs, vs] = (_dot(a_ref[c, p], block_diag(v[rs, vs], dv_first))
                                 + _dot_nt(qe[rs, ps], sin_ref[c, p]))
        return row0

    @pl.when(t < nb)
    def _():
        row0 = scan_block(True)
        ob_ref[pl.ds(row0, tb), :] = o_ref[...]

    @pl.when(t >= nb)
    def _():
        row0 = scan_block(False)
        tot = o_ref[...] + ob_ref[pl.ds(row0, tb), :]
        gr = gr_ref[0]
        for h in range(GLA_HEADS):
            vs = slice(h * GLA_DV, (h + 1) * GLA_DV)
            y = _rms(tot[:, vs], onorm_ref[:, vs])
            gate = gr[:, vs]
            y_ref[0, :, vs] = (y * (gate * jax.nn.sigmoid(gate))).astype(BF16)


def _gla(gq, gk, gv, gfb, gr, up, gbias, onorm, tri, l, *, tb=512):
    b, s, _ = gq.shape
    nb = s // tb
    nc = tb // GLA_CHUNK
    pairs = GLA_HEADS // 2
    assert tri.shape == (2, tb, tb)

    def scan_idx(bi, t):
        return (bi, jnp.where(t < nb, nb - 1 - t, t - nb), 0)

    def fwd_idx(bi, t):
        return (bi, jnp.maximum(t - nb, 0), 0)

    return pl.pallas_call(
        functools.partial(_gla_body, nb=nb, tb=tb),
        grid=(b, 2 * nb),
        in_specs=[pl.BlockSpec((1, tb, GLA_QK_W), scan_idx),
                  pl.BlockSpec((1, tb, GLA_QK_W), scan_idx),
                  pl.BlockSpec((1, tb, GLA_V_W), scan_idx),
                  pl.BlockSpec((1, tb, LANES), scan_idx),
                  pl.BlockSpec((1, tb, GLA_V_W), fwd_idx),
                  _layer(up, l), _layer(gbias, l), _layer(onorm, l), _resident(tri.shape)],
        out_specs=pl.BlockSpec((1, tb, GLA_V_W), fwd_idx),
        out_shape=jax.ShapeDtypeStruct((b, s, GLA_V_W), BF16),
        scratch_shapes=[pltpu.VMEM((s, GLA_V_W), F32),
                        pltpu.VMEM((pairs, 2 * GLA_DV, LANES), F32),
                        pltpu.VMEM((nc, pairs, 2 * GLA_DV, LANES), F32),
                        pltpu.VMEM((nc, pairs, 2 * GLA_DV, LANES), BF16),
                        pltpu.VMEM((nc, pairs, GLA_CHUNK, LANES), BF16),
                        pltpu.VMEM((tb, GLA_V_W), F32)],
        compiler_params=_cparams("arbitrary", "arbitrary"),
        name="gla",
    )(gq, gk, gv, gfb, gr, up, gbias, onorm, tri)


def _gla_tri(tb):
    i = np.arange(tb)
    same = (i[:, None] // GLA_CHUNK) == (i[None, :] // GLA_CHUNK)
    lower = same & (i[None, :] <= i[:, None])
    upper = same & (i[None, :] >= i[:, None])
    return jnp.asarray(np.stack([lower, upper]).astype(np.float32), BF16)


def _mla_attn_body(q_ref, k_ref, v_ref, o_ref, *, tk):
    q = q_ref[0]
    tq = q.shape[0]
    s_len = k_ref.shape[1]
    ones = jnp.ones((tk, LANES), BF16)
    m = jnp.full((tq, 1), -jnp.inf, F32)
    acc = jnp.zeros((tq, MLA_V + LANES), F32)
    for j in range(s_len // tk):
        kb = k_ref[0, j * tk:(j + 1) * tk, :]
        vb = v_ref[0, j * tk:(j + 1) * tk, :]
        s = _dot_nt(q, kb)
        m_new = jnp.maximum(m, jnp.max(s, axis=-1, keepdims=True))
        p = jnp.exp2(s - m_new).astype(BF16)
        acc = jnp.exp2(m - m_new) * acc + _dot(p, jnp.concatenate([vb, ones], axis=1))
        m = m_new
    o_ref[0] = (acc[:, :MLA_V] / acc[:, MLA_V:]).astype(BF16)


def _mla_attention(q, k, v, *, tq=1024, tk=512):
    b, s, _ = q.shape
    return pl.pallas_call(
        functools.partial(_mla_attn_body, tk=tk),
        grid=(b, MLA_HEADS, s // tq),
        in_specs=[pl.BlockSpec((1, tq, MLA_SLAB), lambda bi, h, qi: (bi, qi, h)),
                  pl.BlockSpec((1, s, MLA_SLAB), lambda bi, h, qi: (bi, 0, h)),
                  pl.BlockSpec((1, s, MLA_V), lambda bi, h, qi: (bi, 0, h))],
        out_specs=pl.BlockSpec((1, tq, MLA_V), lambda bi, h, qi: (bi, qi, h)),
        out_shape=jax.ShapeDtypeStruct((b, s, MLA_V_W), BF16),
        compiler_params=_cparams("parallel", "parallel", "arbitrary"),
        name="mla_attn",
    )(q, k, v)


def _merge_body(x_ref, yna_ref, ygla_ref, ymla_ref, gt_ref, wbr_ref, wout_ref, o_ref):
    mixed = None
    for i, y_ref in enumerate((yna_ref, ygla_ref, ymla_ref)):
        br = _dot(y_ref[...], wbr_ref[i])
        term = gt_ref[:, i * D_MODEL:(i + 1) * D_MODEL].astype(F32) * br
        mixed = term if mixed is None else mixed + term
    o_ref[...] = x_ref[...] + _dot(mixed.astype(BF16), wout_ref[...])


def _merge(x, yna, ygla, ymla, gt, wbr, wout, l, *, tm=512):
    n, d = x.shape
    tok = lambda c: pl.BlockSpec((tm, c), lambda i: (i, 0))
    return pl.pallas_call(
        _merge_body,
        grid=(n // tm,),
        in_specs=[tok(d), tok(NA_W), tok(GLA_V_W), tok(MLA_V_W), tok(N_BRANCH * d),
                  _layer(wbr, l), _layer(wout, l)],
        out_specs=tok(d),
        out_shape=jax.ShapeDtypeStruct((n, d), F32),
        compiler_params=_cparams("parallel"),
        name="merge",
    )(x, yna, ygla, ymla, gt, wbr, wout)


def _rope_tables(s):
    half = MLA_ROPE // 2
    inv = ROPE_THETA ** (-jnp.arange(half, dtype=F32) / half)
    ang = jnp.arange(s, dtype=F32)[:, None] * inv[None, :]
    cos, sin = jnp.cos(ang), jnp.sin(ang)
    z = jnp.zeros((s, half), F32)
    pad = jnp.zeros((s, LANES - MLA_ROPE), F32)
    return (jnp.concatenate([cos, cos, pad], axis=1),
            jnp.concatenate([-sin, z, pad], axis=1),
            jnp.concatenate([z, sin, pad], axis=1))


def _pack_w_in(w_in):
    nl, d, _ = w_in.shape
    gfb_end = 3 * NA_W + 2 * GLA_QK_W + 2 * GLA_V_W + 2 * GLA_GATE_RANK
    kr_end = gfb_end + MLA_Q_RANK + MLA_KV_RANK + MLA_ROPE
    z = lambda c: jnp.zeros((nl, d, c), BF16)
    w = w_in.astype(BF16)
    return jnp.concatenate([w[:, :, :gfb_end], z(LANES - 2 * GLA_GATE_RANK),
                            w[:, :, gfb_end:kr_end], z(LANES - MLA_ROPE),
                            w[:, :, kr_end:]], axis=2)


def kernel(x, ffn1_norm, ffn1_w1, ffn1_w3, ffn1_w2, mix_norm, w_in, na_q_norm, na_k_norm, na_rpb, gla_gf_up, gla_gf_bias, gla_gb_up, gla_gb_bias, gla_out_norm, mla_cq_norm, mla_ckv_norm, mla_w_uq, mla_w_ukv, mla_q_norm, mla_k_norm, w_br_na, w_br_gla, w_br_mla, w_out, ffn2_norm, ffn2_w1, ffn2_w3, ffn2_w2):
    b, s, d = x.shape
    nl = w_in.shape[0]
    n = b * s
    gla_tb = 512

    row = lambda a: a[:, None, :].astype(F32)
    f1 = (row(ffn1_norm), ffn1_w1.astype(BF16), ffn1_w3.astype(BF16), ffn1_w2.astype(BF16))
    f2 = (row(ffn2_norm), ffn2_w1.astype(BF16), ffn2_w3.astype(BF16), ffn2_w2.astype(BF16))
    mixg = row(mix_norm)
    w_in_p = _pack_w_in(w_in)
    na_qg = row(jnp.tile(na_q_norm, (1, NA_HEADS)) * (NA_HEAD_DIM ** -0.5))
    na_kg = row(jnp.tile(na_k_norm, (1, NA_HEADS)))
    head_of = np.arange(NA_W) // NA_HEAD_DIM
    hsum = jnp.asarray((head_of[:, None] == head_of[None, :]).astype(np.float32), BF16)
    na_bias = _na_bias_table(na_rpb)

    up = jnp.zeros((nl, LANES, 2 * GLA_QK_W), F32)
    up = up.at[:, :GLA_GATE_RANK, :GLA_QK_W].set(gla_gf_up)
    up = up.at[:, GLA_GATE_RANK:2 * GLA_GATE_RANK, GLA_QK_W:].set(gla_gb_up).astype(BF16)
    gbias = row(jnp.concatenate([gla_gf_bias, gla_gb_bias], axis=1))
    onorm = row(jnp.tile(gla_out_norm, (1, GLA_HEADS)))
    tri = _gla_tri(gla_tb)

    cqg, ckvg = row(mla_cq_norm), row(mla_ckv_norm)
    wuq = mla_w_uq.reshape(nl, MLA_Q_RANK, MLA_HEADS, MLA_QK_HEAD)
    wuq = jnp.pad(wuq, ((0, 0), (0, 0), (0, 0), (0, MLA_SLAB - MLA_QK_HEAD)))
    wuq = wuq.reshape(nl, MLA_Q_RANK, MLA_HEADS * MLA_SLAB).astype(BF16)
    wukv = mla_w_ukv.reshape(nl, MLA_KV_RANK, MLA_HEADS, 2, MLA_NOPE)
    wukv = jnp.swapaxes(wukv, 2, 3).reshape(nl, MLA_KV_RANK, 2 * MLA_HEADS * MLA_NOPE).astype(BF16)
    q_fold = (MLA_QK_HEAD ** -0.5) * math.log2(math.e)
    mqg = row(jnp.pad(mla_q_norm, ((0, 0), (0, MLA_SLAB - MLA_QK_HEAD))) * q_fold)
    mkgn = row(mla_k_norm[:, :MLA_NOPE])
    mkgr = row(jnp.pad(mla_k_norm[:, MLA_NOPE:], ((0, 0), (0, LANES - MLA_ROPE))))
    cos, sa, sb = _rope_tables(s)

    wbr = jnp.stack([w_br_na, w_br_gla, w_br_mla], axis=1).astype(BF16)
    wout = w_out.astype(BF16)

    xf = x.reshape(n, d).astype(F32)
    seq3 = lambda a: a.reshape(b, s, a.shape[-1])
    for l in range(nl):
        xf = _ffn(xf, *f1, l)
        (naq, nak, nav, gq, gk, gv, gr, gfb, mq, mk, mv, gt) = _inproj(
            xf, mixg, w_in_p, na_qg, na_kg, hsum, cqg, ckvg, wuq, wukv, mqg, mkgn, mkgr, cos, sa, sb, l, seq=s)
        y_na = _na_attention(seq3(naq), seq3(nak), seq3(nav), na_bias, l)
        y_gla = _gla(seq3(gq), seq3(gk), seq3(gv), seq3(gfb), seq3(gr), up, gbias, onorm, tri, l, tb=gla_tb)
        y_mla = _mla_attention(seq3(mq), seq3(mk), seq3(mv))
        xf = _merge(xf, y_na.reshape(n, NA_W), y_gla.reshape(n, GLA_V_W), y_mla.reshape(n, MLA_V_W),
                    gt, wbr, wout, l)
        xf = _ffn(xf, *f2, l)
    return xf.reshape(b, s, d).astype(x.dtype)
```

```python
import functools
import math

import numpy as np
import jax
import jax.numpy as jnp
from jax import lax
from jax.experimental import pallas as pl
from jax.experimental.pallas import tpu as pltpu

F32 = jnp.float32
BF16 = jnp.bfloat16

EPS = 1e-6
D_MODEL = 1024
D_FF = 2816
GRID_W = 64
NA_HEADS = 8
NA_HEAD_DIM = 64
NA_WIN_R = 8
NA_WIN_C = 16
NA_W = NA_HEADS * NA_HEAD_DIM
GLA_HEADS = 4
GLA_DK = 64
GLA_DV = 128
GLA_GATE_RANK = 16
GLA_GATE_TAU = 16.0
GLA_CHUNK = 64
GLA_QK_W = GLA_HEADS * GLA_DK
GLA_V_W = GLA_HEADS * GLA_DV
MLA_HEADS = 4
MLA_Q_RANK = 256
MLA_KV_RANK = 256
MLA_NOPE = 128
MLA_ROPE = 64
MLA_V = 128
MLA_QK_HEAD = MLA_NOPE + MLA_ROPE
MLA_SLAB = 256
MLA_V_W = MLA_HEADS * MLA_V
ROPE_THETA = 10000.0
N_BRANCH = 3

LANES = 128
MXU_DIM = 256
C_NA = 0
C_GQ = 3 * NA_W
C_GK = C_GQ + GLA_QK_W
C_GV = C_GK + GLA_QK_W
C_GR = C_GV + GLA_V_W
C_GFB = C_GR + GLA_V_W
C_KR = C_GFB + LANES
C_CQ = C_KR + LANES
C_CKV = C_CQ + MLA_Q_RANK
C_GATE = C_CKV + MLA_KV_RANK
D_IN_PACKED = C_GATE + N_BRANCH * D_MODEL

VMEM_LIMIT = 56 * 1024 * 1024


def _cparams(*sem):
    return pltpu.CompilerParams(dimension_semantics=sem, vmem_limit_bytes=VMEM_LIMIT)


def _resident(shape):
    nd = len(shape)
    return pl.BlockSpec(shape, lambda *_: (0,) * nd, pipeline_mode=pl.Buffered(1))


def _layer(arr, l):
    nd = arr.ndim - 1
    return pl.BlockSpec((None,) + tuple(arr.shape[1:]), lambda *_: (l,) + (0,) * nd,
                        pipeline_mode=pl.Buffered(1))


def _dot(a, b):
    return jnp.dot(a, b, preferred_element_type=F32)


def _dot_nt(a, b):
    return lax.dot_general(a, b, (((1,), (1,)), ((), ())), preferred_element_type=F32)


def _dot_tn(a, b):
    return lax.dot_general(a, b, (((0,), (0,)), ((), ())), preferred_element_type=F32)


def _rms(x, g):
    ms = jnp.mean(x * x, axis=-1, keepdims=True)
    return x * lax.rsqrt(ms + EPS) * g


def _ffn_body(x_ref, g_ref, w1_ref, w3_ref, w2_ref, o_ref, *, ff_bounds):
    x = x_ref[...]
    h = _rms(x, g_ref[...]).astype(BF16)
    acc = None
    for f0, f1 in zip(ff_bounds[:-1], ff_bounds[1:]):
        a = _dot(h, w1_ref[:, f0:f1])
        b = _dot(h, w3_ref[:, f0:f1])
        u = (a * jax.nn.sigmoid(a) * b).astype(BF16)
        part = _dot(u, w2_ref[f0:f1, :])
        acc = part if acc is None else acc + part
    o_ref[...] = x + 0.5 * acc


def _ffn(x, g, w1, w3, w2, l, *, tm=512):
    n, d = x.shape
    tiles = D_FF // MXU_DIM
    ff_bounds = (0, (tiles + 1) // 2 * MXU_DIM, D_FF)
    assert D_FF % MXU_DIM == 0
    return pl.pallas_call(
        functools.partial(_ffn_body, ff_bounds=ff_bounds),
        grid=(n // tm,),
        in_specs=[pl.BlockSpec((tm, d), lambda i: (i, 0)),
                  _layer(g, l), _layer(w1, l), _layer(w3, l), _layer(w2, l)],
        out_specs=pl.BlockSpec((tm, d), lambda i: (i, 0)),
        out_shape=jax.ShapeDtypeStruct((n, d), F32),
        compiler_params=_cparams("parallel"),
        name="ffn",
    )(x, g, w1, w3, w2)


def _mla_qkv(cq, ckv, kr, cqg, ckvg, wuq, wukv, qg, kgn, kgr, cos, sa, sb, q_ref, k_ref, v_ref):
    def rope(slab):
        return slab * cos + pltpu.roll(slab, LANES - MLA_ROPE // 2, 1) * sa + pltpu.roll(slab, MLA_ROPE // 2, 1) * sb

    inv_d = 1.0 / MLA_QK_HEAD
    q = _dot(_rms(cq, cqg).astype(BF16), wuq)
    for h in range(MLA_HEADS):
        qh = q[:, h * MLA_SLAB:(h + 1) * MLA_SLAB]
        ss = jnp.sum(qh * qh, axis=-1, keepdims=True) * inv_d
        qn = qh * lax.rsqrt(ss + EPS) * qg
        q_ref[:, h * MLA_SLAB:h * MLA_SLAB + LANES] = qn[:, :LANES].astype(BF16)
        q_ref[:, h * MLA_SLAB + LANES:(h + 1) * MLA_SLAB] = rope(qn[:, LANES:]).astype(BF16)

    kv = _dot(_rms(ckv, ckvg).astype(BF16), wukv)
    ss_r = jnp.sum(kr * kr, axis=-1, keepdims=True)
    kr_rot = rope(kr * kgr)
    for h in range(MLA_HEADS):
        kn = kv[:, h * MLA_NOPE:(h + 1) * MLA_NOPE]
        ss = (jnp.sum(kn * kn, axis=-1, keepdims=True) + ss_r) * inv_d
        inv = lax.rsqrt(ss + EPS)
        k_ref[:, h * MLA_SLAB:h * MLA_SLAB + LANES] = (kn * inv * kgn).astype(BF16)
        k_ref[:, h * MLA_SLAB + LANES:(h + 1) * MLA_SLAB] = (kr_rot * inv).astype(BF16)
    v_ref[...] = kv[:, MLA_HEADS * MLA_NOPE:].astype(BF16)


def _inproj_body(x_ref, g_ref, wa_ref, wb_ref, wc_ref, wd_ref, qg_ref, kg_ref, hsum_ref,
                 cqg_ref, ckvg_ref, wuq_ref, wukv_ref, mqg_ref, mkgn_ref, mkgr_ref, cos_ref, sa_ref, sb_ref,
                 naq_ref, nak_ref, nav_ref, gq_ref, gk_ref, gv_ref, gr_ref, gfb_ref,
                 mq_ref, mk_ref, mv_ref, gt_ref):
    h = _rms(x_ref[...], g_ref[...]).astype(BF16)

    segments = ((C_NA, wa_ref), (C_GFB, wb_ref), (C_CQ, wc_ref), (C_GATE, wd_ref))

    def proj(c0, n):
        base, ref = [sg for sg in segments if sg[0] <= c0][-1]
        return _dot(h, ref[:, c0 - base:c0 - base + n])

    hsum = hsum_ref[...]

    def head_norm(t, gain):
        ss = _dot((t * t).astype(BF16), hsum)
        return t * lax.rsqrt(ss * (1.0 / NA_HEAD_DIM) + EPS) * gain

    naq_ref[...] = head_norm(proj(C_NA, NA_W), qg_ref[...]).astype(BF16)
    nak_ref[...] = head_norm(proj(C_NA + NA_W, NA_W), kg_ref[...]).astype(BF16)
    nav_ref[...] = proj(C_NA + 2 * NA_W, NA_W).astype(BF16)
    gq_ref[...] = proj(C_GQ, GLA_QK_W)
    gk_ref[...] = proj(C_GK, GLA_QK_W)
    gv_ref[...] = proj(C_GV, GLA_V_W)
    gr_ref[...] = proj(C_GR, GLA_V_W)
    gfb_kr = proj(C_GFB, 2 * LANES)
    gfb_ref[...] = gfb_kr[:, :LANES]
    _mla_qkv(proj(C_CQ, MLA_Q_RANK), proj(C_CKV, MLA_KV_RANK), gfb_kr[:, LANES:],
             cqg_ref[...], ckvg_ref[...], wuq_ref[...], wukv_ref[...],
             mqg_ref[...], mkgn_ref[...], mkgr_ref[...], cos_ref[...], sa_ref[...], sb_ref[...],
             mq_ref, mk_ref, mv_ref)
    for i in range(N_BRANCH):
        c0 = C_GATE + i * D_MODEL
        gt_ref[:, i * D_MODEL:(i + 1) * D_MODEL] = jax.nn.sigmoid(proj(c0, D_MODEL)).astype(BF16)


def _inproj(x, g, w_segs, qg, kg, hsum, cqg, ckvg, wuq, wukv, mqg, mkgn, mkgr, cos, sa, sb, l, *, seq, tm=512):
    n, d = x.shape
    widths = [(NA_W, BF16), (NA_W, BF16), (NA_W, BF16),
              (GLA_QK_W, F32), (GLA_QK_W, F32), (GLA_V_W, F32), (GLA_V_W, F32), (LANES, F32),
              (MLA_HEADS * MLA_SLAB, BF16), (MLA_HEADS * MLA_SLAB, BF16), (MLA_V_W, BF16),
              (N_BRANCH * D_MODEL, BF16)]
    out_specs = [pl.BlockSpec((tm, c), lambda i: (i, 0)) for c, _ in widths]
    out_shape = [jax.ShapeDtypeStruct((n, c), dt) for c, dt in widths]
    pos_blocks = seq // tm
    pos = pl.BlockSpec((tm, LANES), lambda i: (i % pos_blocks, 0))
    return pl.pallas_call(
        _inproj_body,
        grid=(n // tm,),
        in_specs=[pl.BlockSpec((tm, d), lambda i: (i, 0)),
                  _layer(g, l), *[_layer(w, l) for w in w_segs],
                  _layer(qg, l), _layer(kg, l), _resident(hsum.shape),
                  _layer(cqg, l), _layer(ckvg, l), _layer(wuq, l), _layer(wukv, l),
                  _layer(mqg, l), _layer(mkgn, l), _layer(mkgr, l), pos, pos, pos],
        out_specs=out_specs,
        out_shape=out_shape,
        compiler_params=_cparams("parallel"),
        name="inproj",
    )(x, g, *w_segs, qg, kg, hsum, cqg, ckvg, wuq, wukv, mqg, mkgn, mkgr, cos, sa, sb)


def _na_body(q_ref, k_ref, v_ref, tab_ref, o_ref, s_ref, p_ref, bias_ref, *, rb, rows):
    kw_tokens = NA_WIN_R * GRID_W
    pairs = NA_HEADS // 2
    unit_rows = 2 * GRID_W
    lane = lax.broadcasted_iota(jnp.int32, (1, LANES), 1)
    first_head = lane < NA_HEAD_DIM
    r_base = pl.program_id(1) * rb

    @pl.when((pl.program_id(0) == 0) & (pl.program_id(1) == 0))
    def _():
        for d in range(NA_WIN_R):
            off = (NA_WIN_R - 1 - d) * GRID_W
            for p in range(pairs):
                bias_ref[d, p] = tab_ref[p, :, off:off + kw_tokens]

    def window(i):
        r = r_base + i
        r0 = jnp.clip(r - NA_WIN_R // 2, 0, rows - NA_WIN_R)
        return r - r0, pl.ds(pl.multiple_of(r0 * GRID_W, GRID_W), kw_tokens)

    for i in range(rb):
        delta, win = window(i)
        kw = k_ref[0, win, :]
        q = q_ref[0, i * GRID_W:(i + 1) * GRID_W, :]
        for p in range(pairs):
            sl = slice(p * LANES, (p + 1) * LANES)
            qs = q[:, sl]
            zero = jnp.zeros_like(qs)
            lhs = jnp.concatenate([jnp.where(first_head, qs, zero), jnp.where(first_head, zero, qs)], axis=0)
            u = i * pairs + p
            s_ref[u * unit_rows:(u + 1) * unit_rows, :] = _dot_nt(lhs, kw[:, sl]) + bias_ref[delta, p]

    s = s_ref[...]
    p_ref[...] = jnp.exp(s - jnp.max(s, axis=-1, keepdims=True)).astype(BF16)

    ones = jnp.ones((kw_tokens, LANES), BF16)
    for i in range(rb):
        _, win = window(i)
        vw = v_ref[0, win, :]
        for p in range(pairs):
            sl = slice(p * LANES, (p + 1) * LANES)
            u = i * pairs + p
            r = _dot(p_ref[u * unit_rows:(u + 1) * unit_rows, :], jnp.concatenate([vw[:, sl], ones], axis=1))
            o = r[:, :LANES] / r[:, LANES:]
            o_ref[0, i * GRID_W:(i + 1) * GRID_W, sl] = jnp.where(first_head, o[:GRID_W], o[GRID_W:]).astype(BF16)


def _na_attention(q, k, v, bias, l, *, rb=8):
    b, s, w = q.shape
    rows = s // GRID_W
    assert rows >= NA_WIN_R and rows % rb == 0
    units = rb * (NA_HEADS // 2) * 2 * GRID_W
    return pl.pallas_call(
        functools.partial(_na_body, rb=rb, rows=rows),
        grid=(b, rows // rb),
        in_specs=[pl.BlockSpec((1, rb * GRID_W, w), lambda bi, ri: (bi, ri, 0)),
                  pl.BlockSpec((1, s, w), lambda bi, ri: (bi, 0, 0)),
                  pl.BlockSpec((1, s, w), lambda bi, ri: (bi, 0, 0)),
                  _layer(bias, l)],
        out_specs=pl.BlockSpec((1, rb * GRID_W, w), lambda bi, ri: (bi, ri, 0)),
        out_shape=jax.ShapeDtypeStruct((b, s, w), BF16),
        scratch_shapes=[pltpu.VMEM((units, NA_WIN_R * GRID_W), F32),
                        pltpu.VMEM((units, NA_WIN_R * GRID_W), BF16),
                        pltpu.VMEM((NA_WIN_R, NA_HEADS // 2, 2 * GRID_W, NA_WIN_R * GRID_W), F32)],
        compiler_params=_cparams("arbitrary", "arbitrary"),
        name="na_attn",
    )(q, k, v, bias)


def _na_bias_table(rpb):
    nl = rpb.shape[0]
    rel_rows = 2 * NA_WIN_R - 1
    qc = np.arange(GRID_W)
    kc = np.arange(GRID_W)
    dc = np.clip(kc[None, :] - qc[:, None] + (NA_WIN_C - 1), 0, 2 * NA_WIN_C - 2)
    c0 = np.clip(qc - NA_WIN_C // 2, 0, GRID_W - NA_WIN_C)
    ok = (kc[None, :] >= c0[:, None]) & (kc[None, :] < c0[:, None] + NA_WIN_C)
    onehot = jnp.asarray((np.arange(2 * NA_WIN_C - 1)[:, None, None] == dc[None]).astype(np.float32))
    tab = jnp.einsum("lhrc,cqk->lhqrk", rpb.astype(F32), onehot, precision=lax.Precision.HIGHEST)
    tab = jnp.where(ok[None, None, :, None, :], tab, -1e30)
    tab = tab.reshape(nl, NA_HEADS // 2, 2 * GRID_W, rel_rows * GRID_W)
    return jnp.pad(tab, ((0, 0), (0, 0), (0, 0), (0, GRID_W)))


def _gla_body(gq_ref, gk_ref, gv_ref, gfb_ref, gr_ref, up_ref, gbias_ref, onorm_ref, tri_ref,
              y_ref, ob_ref, st_ref, upd_ref, sin_ref, a_ref, o_ref, *, nb, tb):
    t = pl.program_id(1)
    nc = tb // GLA_CHUNK
    pairs = GLA_HEADS // 2
    pair_dv = 2 * GLA_DV
    dk_first = lax.broadcasted_iota(jnp.int32, (1, LANES), 1) < GLA_DK
    dv_first = lax.broadcasted_iota(jnp.int32, (1, pair_dv), 1) < GLA_DV
    ci = lax.broadcasted_iota(jnp.int32, (GLA_CHUNK, LANES), 0)
    cj = lax.broadcasted_iota(jnp.int32, (GLA_CHUNK, LANES), 1) % GLA_CHUNK
    same_head = (lax.broadcasted_iota(jnp.int32, (pair_dv, LANES), 0) < GLA_DV) == dk_first

    @pl.when((t == 0) | (t == nb))
    def _():
        st_ref[...] = jnp.zeros_like(st_ref)

    def scan_block(bwd):
        blk = (nb - 1 - t) if bwd else (t - nb)
        row0 = pl.multiple_of(blk * tb, tb)
        gsl = slice(GLA_QK_W, 2 * GLA_QK_W) if bwd else slice(0, GLA_QK_W)
        x = _dot(gfb_ref[0].astype(BF16), up_ref[:, gsl]) + gbias_ref[:, gsl]
        g = (jnp.minimum(x, 0.0) - jnp.log(1.0 + jnp.exp(-jnp.abs(x)))) * (1.0 / GLA_GATE_TAU)
        g_hi = g.astype(BF16)
        g_lo = (g - g_hi.astype(F32)).astype(BF16)
        tri = tri_ref[1 if bwd else 0]
        cum = _dot(tri, g_hi) + _dot(tri, g_lo)
        k = gk_ref[0]
        qe = (gq_ref[0] * (GLA_DK ** -0.5) * jnp.exp(cum)).astype(BF16)
        ke = (k * jnp.exp(-cum)).astype(BF16)
        v = gv_ref[0].astype(BF16)
        keep = (cj > ci) if bwd else (cj <= ci)
        order = list(range(nc - 1, -1, -1) if bwd else range(nc))

        def rows_of(c):
            return slice(c * GLA_CHUNK, (c + 1) * GLA_CHUNK)

        def lanes_of(p):
            return slice(p * LANES, (p + 1) * LANES), slice(p * pair_dv, (p + 1) * pair_dv)

        def block_diag(x, first):
            zero = jnp.zeros_like(x)
            return jnp.concatenate([jnp.where(first, x, zero), jnp.where(first, zero, x)], axis=0)

        decays = {}
        for c in order:
            rs = rows_of(c)
            edge = c * GLA_CHUNK if bwd else (c + 1) * GLA_CHUNK - 1
            cum_end = cum[edge:edge + 1, :]
            k_end = (k[rs] * jnp.exp(cum_end - cum[rs])).astype(BF16)
            decays[c] = jnp.exp(cum_end)
            for p in range(pairs):
                ps, vs = lanes_of(p)
                scores = _dot_nt(qe[rs, ps], block_diag(ke[rs, ps], dk_first))
                a_ref[c, p] = jnp.where(keep, scores, 0.0).astype(BF16)
                inc = _dot_tn(v[rs, vs], k_end[:, ps])
                upd_ref[c, p] = jnp.where(same_head, inc, 0.0)
        for p in range(pairs):
            ps, _ = lanes_of(p)
            st = st_ref[p]
            for c in order:
                sin_ref[c, p] = st.astype(BF16)
                st = decays[c][:, ps] * st + upd_ref[c, p]
            st_ref[p] = st
        for c in order:
            rs = rows_of(c)
            for p in range(pairs):
                ps, vs = lanes_of(p)
                o_ref[rs, vs] = (_dot(a_ref[c, p], block_diag(v[rs, vs], dv_first))
                                 + _dot_nt(qe[rs, ps], sin_ref[c, p]))
        return row0

    @pl.when(t < nb)
    def _():
        row0 = scan_block(True)
        ob_ref[pl.ds(row0, tb), :] = o_ref[...]

    @pl.when(t >= nb)
    def _():
        row0 = scan_block(False)
        tot = o_ref[...] + ob_ref[pl.ds(row0, tb), :]
        gr = gr_ref[0]
        for h in range(GLA_HEADS):
            vs = slice(h * GLA_DV, (h + 1) * GLA_DV)
            y = _rms(tot[:, vs], onorm_ref[:, vs])
            gate = gr[:, vs]
            y_ref[0, :, vs] = (y * (gate * jax.nn.sigmoid(gate))).astype(BF16)


def _gla(gq, gk, gv, gfb, gr, up, gbias, onorm, tri, l, *, tb=512):
    b, s, _ = gq.shape
    nb = s // tb
    nc = tb // GLA_CHUNK
    pairs = GLA_HEADS // 2
    assert tri.shape == (2, tb, tb)

    def scan_idx(bi, t):
        return (bi, jnp.where(t < nb, nb - 1 - t, t - nb), 0)

    def fwd_idx(bi, t):
        return (bi, jnp.maximum(t - nb, 0), 0)

    return pl.pallas_call(
        functools.partial(_gla_body, nb=nb, tb=tb),
        grid=(b, 2 * nb),
        in_specs=[pl.BlockSpec((1, tb, GLA_QK_W), scan_idx),
                  pl.BlockSpec((1, tb, GLA_QK_W), scan_idx),
                  pl.BlockSpec((1, tb, GLA_V_W), scan_idx),
                  pl.BlockSpec((1, tb, LANES), scan_idx),
                  pl.BlockSpec((1, tb, GLA_V_W), fwd_idx),
                  _layer(up, l), _layer(gbias, l), _layer(onorm, l), _resident(tri.shape)],
        out_specs=pl.BlockSpec((1, tb, GLA_V_W), fwd_idx),
        out_shape=jax.ShapeDtypeStruct((b, s, GLA_V_W), BF16),
        scratch_shapes=[pltpu.VMEM((s, GLA_V_W), F32),
                        pltpu.VMEM((pairs, 2 * GLA_DV, LANES), F32),
                        pltpu.VMEM((nc, pairs, 2 * GLA_DV, LANES), F32),
                        pltpu.VMEM((nc, pairs, 2 * GLA_DV, LANES), BF16),
                        pltpu.VMEM((nc, pairs, GLA_CHUNK, LANES), BF16),
                        pltpu.VMEM((tb, GLA_V_W), F32)],
        compiler_params=_cparams("arbitrary", "arbitrary"),
        name="gla",
    )(gq, gk, gv, gfb, gr, up, gbias, onorm, tri)


def _gla_tri(tb):
    i = np.arange(tb)
    same = (i[:, None] // GLA_CHUNK) == (i[None, :] // GLA_CHUNK)
    lower = same & (i[None, :] <= i[:, None])
    upper = same & (i[None, :] >= i[:, None])
    return jnp.asarray(np.stack([lower, upper]).astype(np.float32), BF16)


def _mla_attn_body(q_ref, k_ref, v_ref, o_ref, *, tk):
    q = q_ref[0]
    tq = q.shape[0]
    s_len = k_ref.shape[1]
    ones = jnp.ones((tk, LANES), BF16)
    m = jnp.full((tq, 1), -jnp.inf, F32)
    acc = jnp.zeros((tq, MLA_V + LANES), F32)
    for j in range(s_len // tk):
        kb = k_ref[0, j * tk:(j + 1) * tk, :]
        vb = v_ref[0, j * tk:(j + 1) * tk, :]
        s = _dot_nt(q, kb)
        m_new = jnp.maximum(m, jnp.max(s, axis=-1, keepdims=True))
        p = jnp.exp2(s - m_new).astype(BF16)
        acc = jnp.exp2(m - m_new) * acc + _dot(p, jnp.concatenate([vb, ones], axis=1))
        m = m_new
    o_ref[0] = (acc[:, :MLA_V] / acc[:, MLA_V:]).astype(BF16)


def _mla_attention(q, k, v, *, tq=1024, tk=512):
    b, s, _ = q.shape
    return pl.pallas_call(
        functools.partial(_mla_attn_body, tk=tk),
        grid=(b, MLA_HEADS, s // tq),
        in_specs=[pl.BlockSpec((1, tq, MLA_SLAB), lambda bi, h, qi: (bi, qi, h)),
                  pl.BlockSpec((1, s, MLA_SLAB), lambda bi, h, qi: (bi, 0, h)),
                  pl.BlockSpec((1, s, MLA_V), lambda bi, h, qi: (bi, 0, h))],
        out_specs=pl.BlockSpec((1, tq, MLA_V), lambda bi, h, qi: (bi, qi, h)),
        out_shape=jax.ShapeDtypeStruct((b, s, MLA_V_W), BF16),
        compiler_params=_cparams("parallel", "parallel", "arbitrary"),
        name="mla_attn",
    )(q, k, v)


def _merge_body(x_ref, yna_ref, ygla_ref, ymla_ref, gt_ref, wbr_ref, wout_ref, o_ref):
    mixed = None
    for i, y_ref in enumerate((yna_ref, ygla_ref, ymla_ref)):
        br = _dot(y_ref[...], wbr_ref[i])
        term = gt_ref[:, i * D_MODEL:(i + 1) * D_MODEL].astype(F32) * br
        mixed = term if mixed is None else mixed + term
    o_ref[...] = x_ref[...] + _dot(mixed.astype(BF16), wout_ref[...])


def _merge(x, yna, ygla, ymla, gt, wbr, wout, l, *, tm=512):
    n, d = x.shape
    tok = lambda c: pl.BlockSpec((tm, c), lambda i: (i, 0))
    return pl.pallas_call(
        _merge_body,
        grid=(n // tm,),
        in_specs=[tok(d), tok(NA_W), tok(GLA_V_W), tok(MLA_V_W), tok(N_BRANCH * d),
                  _layer(wbr, l), _layer(wout, l)],
        out_specs=tok(d),
        out_shape=jax.ShapeDtypeStruct((n, d), F32),
        compiler_params=_cparams("parallel"),
        name="merge",
    )(x, yna, ygla, ymla, gt, wbr, wout)


def _rope_tables(s):
    half = MLA_ROPE // 2
    inv = ROPE_THETA ** (-jnp.arange(half, dtype=F32) / half)
    ang = jnp.arange(s, dtype=F32)[:, None] * inv[None, :]
    cos, sin = jnp.cos(ang), jnp.sin(ang)
    z = jnp.zeros((s, half), F32)
    pad = jnp.zeros((s, LANES - MLA_ROPE), F32)
    return (jnp.concatenate([cos, cos, pad], axis=1),
            jnp.concatenate([-sin, z, pad], axis=1),
            jnp.concatenate([z, sin, pad], axis=1))


def _pack_w_in(w_in):
    nl, d, _ = w_in.shape
    gfb_end = 3 * NA_W + 2 * GLA_QK_W + 2 * GLA_V_W + 2 * GLA_GATE_RANK
    kr_end = gfb_end + MLA_Q_RANK + MLA_KV_RANK + MLA_ROPE
    z = lambda c: jnp.zeros((nl, d, c), BF16)
    kr_start = kr_end - MLA_ROPE
    cols = lambda a, b: w_in[:, :, a:b].astype(BF16)
    slabs = jnp.concatenate([cols(C_GFB, gfb_end), z(LANES - 2 * GLA_GATE_RANK),
                             cols(kr_start, kr_end), z(LANES - MLA_ROPE)], axis=2)
    return (cols(0, C_GFB), slabs, cols(gfb_end, kr_start), cols(kr_end, w_in.shape[2]))


def kernel(x, ffn1_norm, ffn1_w1, ffn1_w3, ffn1_w2, mix_norm, w_in, na_q_norm, na_k_norm, na_rpb, gla_gf_up, gla_gf_bias, gla_gb_up, gla_gb_bias, gla_out_norm, mla_cq_norm, mla_ckv_norm, mla_w_uq, mla_w_ukv, mla_q_norm, mla_k_norm, w_br_na, w_br_gla, w_br_mla, w_out, ffn2_norm, ffn2_w1, ffn2_w3, ffn2_w2):
    b, s, d = x.shape
    nl = w_in.shape[0]
    n = b * s
    gla_tb = 512

    row = lambda a: a[:, None, :].astype(F32)
    f1 = (row(ffn1_norm), ffn1_w1.astype(BF16), ffn1_w3.astype(BF16), ffn1_w2.astype(BF16))
    f2 = (row(ffn2_norm), ffn2_w1.astype(BF16), ffn2_w3.astype(BF16), ffn2_w2.astype(BF16))
    mixg = row(mix_norm)
    w_in_p = _pack_w_in(w_in)
    na_qg = row(jnp.tile(na_q_norm, (1, NA_HEADS)) * (NA_HEAD_DIM ** -0.5))
    na_kg = row(jnp.tile(na_k_norm, (1, NA_HEADS)))
    head_of = np.arange(NA_W) // NA_HEAD_DIM
    hsum = jnp.asarray((head_of[:, None] == head_of[None, :]).astype(np.float32), BF16)
    na_bias = _na_bias_table(na_rpb)

    up = jnp.zeros((nl, LANES, 2 * GLA_QK_W), F32)
    up = up.at[:, :GLA_GATE_RANK, :GLA_QK_W].set(gla_gf_up)
    up = up.at[:, GLA_GATE_RANK:2 * GLA_GATE_RANK, GLA_QK_W:].set(gla_gb_up).astype(BF16)
    gbias = row(jnp.concatenate([gla_gf_bias, gla_gb_bias], axis=1))
    onorm = row(jnp.tile(gla_out_norm, (1, GLA_HEADS)))
    tri = _gla_tri(gla_tb)

    cqg, ckvg = row(mla_cq_norm), row(mla_ckv_norm)
    wuq = mla_w_uq.reshape(nl, MLA_Q_RANK, MLA_HEADS, MLA_QK_HEAD)
    wuq = jnp.pad(wuq, ((0, 0), (0, 0), (0, 0), (0, MLA_SLAB - MLA_QK_HEAD)))
    wuq = wuq.reshape(nl, MLA_Q_RANK, MLA_HEADS * MLA_SLAB).astype(BF16)
    wukv = mla_w_ukv.reshape(nl, MLA_KV_RANK, MLA_HEADS, 2, MLA_NOPE)
    wukv = jnp.swapaxes(wukv, 2, 3).reshape(nl, MLA_KV_RANK, 2 * MLA_HEADS * MLA_NOPE).astype(BF16)
    q_fold = (MLA_QK_HEAD ** -0.5) * math.log2(math.e)
    mqg = row(jnp.pad(mla_q_norm, ((0, 0), (0, MLA_SLAB - MLA_QK_HEAD))) * q_fold)
    mkgn = row(mla_k_norm[:, :MLA_NOPE])
    mkgr = row(jnp.pad(mla_k_norm[:, MLA_NOPE:], ((0, 0), (0, LANES - MLA_ROPE))))
    cos, sa, sb = _rope_tables(s)

    wbr = jnp.stack([w_br_na, w_br_gla, w_br_mla], axis=1).astype(BF16)
    wout = w_out.astype(BF16)

    xf = x.reshape(n, d).astype(F32)
    seq3 = lambda a: a.reshape(b, s, a.shape[-1])
    for l in range(nl):
        xf = _ffn(xf, *f1, l)
        (naq, nak, nav, gq, gk, gv, gr, gfb, mq, mk, mv, gt) = _inproj(
            xf, mixg, w_in_p, na_qg, na_kg, hsum, cqg, ckvg, wuq, wukv, mqg, mkgn, mkgr, cos, sa, sb, l, seq=s)
        y_na = _na_attention(seq3(naq), seq3(nak), seq3(nav), na_bias, l)
        y_gla = _gla(seq3(gq), seq3(gk), seq3(gv), seq3(gfb), seq3(gr), up, gbias, onorm, tri, l, tb=gla_tb)
        y_mla = _mla_attention(seq3(mq), seq3(mk), seq3(mv))
        xf = _merge(xf, y_na.reshape(n, NA_W), y_gla.reshape(n, GLA_V_W), y_mla.reshape(n, MLA_V_W),
                    gt, wbr, wout, l)
        xf = _ffn(xf, *f2, l)
    return xf.reshape(b, s, d).astype(x.dtype)
```

```python
import functools
import math

import numpy as np
import jax
import jax.numpy as jnp
from jax import lax
from jax.experimental import pallas as pl
from jax.experimental.pallas import tpu as pltpu

F32 = jnp.float32
BF16 = jnp.bfloat16

EPS = 1e-6
D_MODEL = 1024
D_FF = 2816
GRID_W = 64
NA_HEADS = 8
NA_HEAD_DIM = 64
NA_WIN_R = 8
NA_WIN_C = 16
NA_W = NA_HEADS * NA_HEAD_DIM
GLA_HEADS = 4
GLA_DK = 64
GLA_DV = 128
GLA_GATE_RANK = 16
GLA_GATE_TAU = 16.0
GLA_CHUNK = 64
GLA_QK_W = GLA_HEADS * GLA_DK
GLA_V_W = GLA_HEADS * GLA_DV
MLA_HEADS = 4
MLA_Q_RANK = 256
MLA_KV_RANK = 256
MLA_NOPE = 128
MLA_ROPE = 64
MLA_V = 128
MLA_QK_HEAD = MLA_NOPE + MLA_ROPE
MLA_SLAB = 256
MLA_V_W = MLA_HEADS * MLA_V
ROPE_THETA = 10000.0
N_BRANCH = 3

LANES = 128
MXU_DIM = 256
C_NA = 0
C_GQ = 3 * NA_W
C_GK = C_GQ + GLA_QK_W
C_GV = C_GK + GLA_QK_W
C_GR = C_GV + GLA_V_W
C_GFB = C_GR + GLA_V_W
C_KR = C_GFB + LANES
C_CQ = C_KR + LANES
C_CKV = C_CQ + MLA_Q_RANK
C_GATE = C_CKV + MLA_KV_RANK
D_IN_PACKED = C_GATE + N_BRANCH * D_MODEL

VMEM_LIMIT = 56 * 1024 * 1024


def _cparams(*sem):
    return pltpu.CompilerParams(dimension_semantics=sem, vmem_limit_bytes=VMEM_LIMIT)


def _resident(shape):
    nd = len(shape)
    return pl.BlockSpec(shape, lambda *_: (0,) * nd, pipeline_mode=pl.Buffered(1))


def _layer(arr, l):
    nd = arr.ndim - 1
    return pl.BlockSpec((None,) + tuple(arr.shape[1:]), lambda *_: (l,) + (0,) * nd,
                        pipeline_mode=pl.Buffered(1))


def _dot(a, b):
    return jnp.dot(a, b, preferred_element_type=F32)


def _dot_nt(a, b):
    return lax.dot_general(a, b, (((1,), (1,)), ((), ())), preferred_element_type=F32)


def _dot_tn(a, b):
    return lax.dot_general(a, b, (((0,), (0,)), ((), ())), preferred_element_type=F32)


def _rms(x, g):
    ms = jnp.mean(x * x, axis=-1, keepdims=True)
    return x * lax.rsqrt(ms + EPS) * g


def _ffn_body(x_ref, g_ref, w1_ref, w3_ref, w2_ref, o_ref, *, ff_bounds):
    x = x_ref[...]
    h = _rms(x, g_ref[...]).astype(BF16)
    acc = None
    for f0, f1 in zip(ff_bounds[:-1], ff_bounds[1:]):
        a = _dot(h, w1_ref[:, f0:f1])
        b = _dot(h, w3_ref[:, f0:f1])
        u = (a * jax.nn.sigmoid(a) * b).astype(BF16)
        part = _dot(u, w2_ref[f0:f1, :])
        acc = part if acc is None else acc + part
    o_ref[...] = x + 0.5 * acc


def _ffn(x, g, w1, w3, w2, l, *, tm=512):
    n, d = x.shape
    tiles = D_FF // MXU_DIM
    ff_bounds = (0, (tiles + 1) // 2 * MXU_DIM, D_FF)
    assert D_FF % MXU_DIM == 0 and n % tm == 0
    return pl.pallas_call(
        functools.partial(_ffn_body, ff_bounds=ff_bounds),
        grid=(n // tm,),
        in_specs=[pl.BlockSpec((tm, d), lambda i: (i, 0)),
                  _layer(g, l), _layer(w1, l), _layer(w3, l), _layer(w2, l)],
        out_specs=pl.BlockSpec((tm, d), lambda i: (i, 0)),
        out_shape=jax.ShapeDtypeStruct((n, d), F32),
        compiler_params=_cparams("parallel"),
        name="ffn",
    )(x, g, w1, w3, w2)


def _mla_qkv(cq, ckv, kr, cqg, ckvg, wuq, wukv, qg, kgn, kgr, cos, sa, sb, q_ref, k_ref, v_ref):
    def rope(slab):
        return slab * cos + pltpu.roll(slab, LANES - MLA_ROPE // 2, 1) * sa + pltpu.roll(slab, MLA_ROPE // 2, 1) * sb

    inv_d = 1.0 / MLA_QK_HEAD
    q = _dot(_rms(cq, cqg).astype(BF16), wuq)
    for h in range(MLA_HEADS):
        qh = q[:, h * MLA_SLAB:(h + 1) * MLA_SLAB]
        ss = jnp.sum(qh * qh, axis=-1, keepdims=True) * inv_d
        qn = qh * lax.rsqrt(ss + EPS) * qg
        q_ref[:, h * MLA_SLAB:h * MLA_SLAB + LANES] = qn[:, :LANES].astype(BF16)
        q_ref[:, h * MLA_SLAB + LANES:(h + 1) * MLA_SLAB] = rope(qn[:, LANES:]).astype(BF16)

    kv = _dot(_rms(ckv, ckvg).astype(BF16), wukv)
    ss_r = jnp.sum(kr * kr, axis=-1, keepdims=True)
    kr_rot = rope(kr * kgr)
    for h in range(MLA_HEADS):
        kn = kv[:, h * MLA_NOPE:(h + 1) * MLA_NOPE]
        ss = (jnp.sum(kn * kn, axis=-1, keepdims=True) + ss_r) * inv_d
        inv = lax.rsqrt(ss + EPS)
        k_ref[:, h * MLA_SLAB:h * MLA_SLAB + LANES] = (kn * inv * kgn).astype(BF16)
        k_ref[:, h * MLA_SLAB + LANES:(h + 1) * MLA_SLAB] = (kr_rot * inv).astype(BF16)
    v_ref[...] = kv[:, MLA_HEADS * MLA_NOPE:].astype(BF16)


def _inproj_body(x_ref, g_ref, wa_ref, wb_ref, wd_ref, qg_ref, kg_ref, hsum_ref,
                 cqg_ref, ckvg_ref, wuq_ref, wukv_ref, mqg_ref, mkgn_ref, mkgr_ref, cos_ref, sa_ref, sb_ref,
                 naq_ref, nak_ref, nav_ref, gq_ref, gk_ref, gv_ref, gr_ref, gfb_ref,
                 mq_ref, mk_ref, mv_ref, gt_ref):
    h = _rms(x_ref[...], g_ref[...]).astype(BF16)
    hsum = hsum_ref[...]

    def head_norm(t, gain):
        ss = _dot((t * t).astype(BF16), hsum)
        return t * lax.rsqrt(ss * (1.0 / NA_HEAD_DIM) + EPS) * gain

    na = _dot(h, wa_ref[:, C_NA:C_GQ])
    naq_ref[...] = head_norm(na[:, :NA_W], qg_ref[...]).astype(BF16)
    nak_ref[...] = head_norm(na[:, NA_W:2 * NA_W], kg_ref[...]).astype(BF16)
    nav_ref[...] = na[:, 2 * NA_W:].astype(BF16)
    gla = _dot(h, wa_ref[:, C_GQ:C_GFB])
    gq_ref[...] = gla[:, C_GQ - C_GQ:C_GK - C_GQ]
    gk_ref[...] = gla[:, C_GK - C_GQ:C_GV - C_GQ]
    gv_ref[...] = gla[:, C_GV - C_GQ:C_GR - C_GQ]
    gr_ref[...] = gla[:, C_GR - C_GQ:]
    lat = _dot(h, wb_ref[...])
    gfb_ref[...] = lat[:, :LANES]
    _mla_qkv(lat[:, C_CQ - C_GFB:C_CKV - C_GFB], lat[:, C_CKV - C_GFB:], lat[:, C_KR - C_GFB:C_CQ - C_GFB],
             cqg_ref[...], ckvg_ref[...], wuq_ref[...], wukv_ref[...],
             mqg_ref[...], mkgn_ref[...], mkgr_ref[...], cos_ref[...], sa_ref[...], sb_ref[...],
             mq_ref, mk_ref, mv_ref)
    gt_ref[...] = jax.nn.sigmoid(_dot(h, wd_ref[...])).astype(BF16)


def _inproj(x, g, w_segs, qg, kg, hsum, cqg, ckvg, wuq, wukv, mqg, mkgn, mkgr, cos, sa, sb, l, *, seq, tm=512):
    n, d = x.shape
    widths = [(NA_W, BF16), (NA_W, BF16), (NA_W, BF16),
              (GLA_QK_W, F32), (GLA_QK_W, F32), (GLA_V_W, F32), (GLA_V_W, F32), (LANES, F32),
              (MLA_HEADS * MLA_SLAB, BF16), (MLA_HEADS * MLA_SLAB, BF16), (MLA_V_W, BF16),
              (N_BRANCH * D_MODEL, BF16)]
    out_specs = [pl.BlockSpec((tm, c), lambda i: (i, 0)) for c, _ in widths]
    out_shape = [jax.ShapeDtypeStruct((n, c), dt) for c, dt in widths]
    pos_blocks = seq // tm
    pos = pl.BlockSpec((tm, LANES), lambda i: (i % pos_blocks, 0))
    return pl.pallas_call(
        _inproj_body,
        grid=(n // tm,),
        in_specs=[pl.BlockSpec((tm, d), lambda i: (i, 0)),
                  _layer(g, l), *[_layer(w, l) for w in w_segs],
                  _layer(qg, l), _layer(kg, l), _resident(hsum.shape),
                  _layer(cqg, l), _layer(ckvg, l), _layer(wuq, l), _layer(wukv, l),
                  _layer(mqg, l), _layer(mkgn, l), _layer(mkgr, l), pos, pos, pos],
        out_specs=out_specs,
        out_shape=out_shape,
        compiler_params=_cparams("parallel"),
        name="inproj",
    )(x, g, *w_segs, qg, kg, hsum, cqg, ckvg, wuq, wukv, mqg, mkgn, mkgr, cos, sa, sb)


def _na_body(q_ref, k_ref, v_ref, tab_ref, o_ref, s_ref, p_ref, bias_ref, *, rb, rows):
    kw_tokens = NA_WIN_R * GRID_W
    pairs = NA_HEADS // 2
    unit_rows = 2 * GRID_W
    lane = lax.broadcasted_iota(jnp.int32, (1, LANES), 1)
    first_head = lane < NA_HEAD_DIM
    r_base = pl.program_id(1) * rb

    @pl.when((pl.program_id(0) == 0) & (pl.program_id(1) == 0))
    def _():
        for d in range(NA_WIN_R):
            off = (NA_WIN_R - 1 - d) * GRID_W
            for p in range(pairs):
                bias_ref[d, p] = tab_ref[p, :, off:off + kw_tokens]

    def window(i):
        r = r_base + i
        r0 = jnp.clip(r - NA_WIN_R // 2, 0, rows - NA_WIN_R)
        return r - r0, pl.ds(pl.multiple_of(r0 * GRID_W, GRID_W), kw_tokens)

    for i in range(rb):
        delta, win = window(i)
        kw = k_ref[0, win, :]
        q = q_ref[0, i * GRID_W:(i + 1) * GRID_W, :]
        for p in range(pairs):
            sl = slice(p * LANES, (p + 1) * LANES)
            qs = q[:, sl]
            zero = jnp.zeros_like(qs)
            lhs = jnp.concatenate([jnp.where(first_head, qs, zero), jnp.where(first_head, zero, qs)], axis=0)
            u = i * pairs + p
            s_ref[u * unit_rows:(u + 1) * unit_rows, :] = _dot_nt(lhs, kw[:, sl]) + bias_ref[delta, p]

    s = s_ref[...]
    p_ref[...] = jnp.exp(s - jnp.max(s, axis=-1, keepdims=True)).astype(BF16)

    ones = jnp.ones((kw_tokens, LANES), BF16)
    for i in range(rb):
        _, win = window(i)
        vw = v_ref[0, win, :]
        for p in range(pairs):
            sl = slice(p * LANES, (p + 1) * LANES)
            u = i * pairs + p
            r = _dot(p_ref[u * unit_rows:(u + 1) * unit_rows, :], jnp.concatenate([vw[:, sl], ones], axis=1))
            o = r[:, :LANES] / r[:, LANES:]
            o_ref[0, i * GRID_W:(i + 1) * GRID_W, sl] = jnp.where(first_head, o[:GRID_W], o[GRID_W:]).astype(BF16)


def _na_attention(q, k, v, bias, l, *, rb=8):
    b, s, w = q.shape
    rows = s // GRID_W
    assert rows >= NA_WIN_R and rows % rb == 0
    units = rb * (NA_HEADS // 2) * 2 * GRID_W
    return pl.pallas_call(
        functools.partial(_na_body, rb=rb, rows=rows),
        grid=(b, rows // rb),
        in_specs=[pl.BlockSpec((1, rb * GRID_W, w), lambda bi, ri: (bi, ri, 0)),
                  pl.BlockSpec((1, s, w), lambda bi, ri: (bi, 0, 0)),
                  pl.BlockSpec((1, s, w), lambda bi, ri: (bi, 0, 0)),
                  _layer(bias, l)],
        out_specs=pl.BlockSpec((1, rb * GRID_W, w), lambda bi, ri: (bi, ri, 0)),
        out_shape=jax.ShapeDtypeStruct((b, s, w), BF16),
        scratch_shapes=[pltpu.VMEM((units, NA_WIN_R * GRID_W), F32),
                        pltpu.VMEM((units, NA_WIN_R * GRID_W), BF16),
                        pltpu.VMEM((NA_WIN_R, NA_HEADS // 2, 2 * GRID_W, NA_WIN_R * GRID_W), F32)],
        compiler_params=_cparams("arbitrary", "arbitrary"),
        name="na_attn",
    )(q, k, v, bias)


def _na_bias_table(rpb):
    nl = rpb.shape[0]
    rel_rows = 2 * NA_WIN_R - 1
    qc = np.arange(GRID_W)
    kc = np.arange(GRID_W)
    dc = np.clip(kc[None, :] - qc[:, None] + (NA_WIN_C - 1), 0, 2 * NA_WIN_C - 2)
    c0 = np.clip(qc - NA_WIN_C // 2, 0, GRID_W - NA_WIN_C)
    ok = (kc[None, :] >= c0[:, None]) & (kc[None, :] < c0[:, None] + NA_WIN_C)
    onehot = jnp.asarray((np.arange(2 * NA_WIN_C - 1)[:, None, None] == dc[None]).astype(np.float32))
    tab = jnp.einsum("lhrc,cqk->lhqrk", rpb.astype(F32), onehot, precision=lax.Precision.HIGHEST)
    tab = jnp.where(ok[None, None, :, None, :], tab, -1e30)
    tab = tab.reshape(nl, NA_HEADS // 2, 2 * GRID_W, rel_rows * GRID_W)
    return jnp.pad(tab, ((0, 0), (0, 0), (0, 0), (0, GRID_W)))


def _gla_body(gq_ref, gk_ref, gv_ref, gfb_ref, gr_ref, up_ref, gbias_ref, onorm_ref, tri_ref,
              y_ref, ob_ref, st_ref, upd_ref, sin_ref, a_ref, o_ref, *, nb, tb):
    t = pl.program_id(1)
    nc = tb // GLA_CHUNK
    pairs = GLA_HEADS // 2
    pair_dv = 2 * GLA_DV
    dk_first = lax.broadcasted_iota(jnp.int32, (1, LANES), 1) < GLA_DK
    dv_first = lax.broadcasted_iota(jnp.int32, (1, pair_dv), 1) < GLA_DV
    ci = lax.broadcasted_iota(jnp.int32, (GLA_CHUNK, LANES), 0)
    cj = lax.broadcasted_iota(jnp.int32, (GLA_CHUNK, LANES), 1) % GLA_CHUNK
    same_head = (lax.broadcasted_iota(jnp.int32, (pair_dv, LANES), 0) < GLA_DV) == dk_first

    @pl.when((t == 0) | (t == nb))
    def _():
        st_ref[...] = jnp.zeros_like(st_ref)

    def scan_block(bwd):
        blk = (nb - 1 - t) if bwd else (t - nb)
        row0 = pl.multiple_of(blk * tb, tb)
        gsl = slice(GLA_QK_W, 2 * GLA_QK_W) if bwd else slice(0, GLA_QK_W)
        x = _dot(gfb_ref[0].astype(BF16), up_ref[:, gsl]) + gbias_ref[:, gsl]
        g = (jnp.minimum(x, 0.0) - jnp.log(1.0 + jnp.exp(-jnp.abs(x)))) * (1.0 / GLA_GATE_TAU)
        g_hi = g.astype(BF16)
        g_lo = (g - g_hi.astype(F32)).astype(BF16)
        tri = tri_ref[1 if bwd else 0]
        cum = _dot(tri, g_hi) + _dot(tri, g_lo)
        k = gk_ref[0]
        qe = (gq_ref[0] * (GLA_DK ** -0.5) * jnp.exp(cum)).astype(BF16)
        ke = (k * jnp.exp(-cum)).astype(BF16)
        v = gv_ref[0].astype(BF16)
        keep = (cj > ci) if bwd else (cj <= ci)
        order = list(range(nc - 1, -1, -1) if bwd else range(nc))

        def rows_of(c):
            return slice(c * GLA_CHUNK, (c + 1) * GLA_CHUNK)

        def lanes_of(p):
            return slice(p * LANES, (p + 1) * LANES), slice(p * pair_dv, (p + 1) * pair_dv)

        def block_diag(x, first):
            zero = jnp.zeros_like(x)
            return jnp.concatenate([jnp.where(first, x, zero), jnp.where(first, zero, x)], axis=0)

        decays = {}
        for c in order:
            rs = rows_of(c)
            edge = c * GLA_CHUNK if bwd else (c + 1) * GLA_CHUNK - 1
            cum_end = cum[edge:edge + 1, :]
            k_end = (k[rs] * jnp.exp(cum_end - cum[rs])).astype(BF16)
            decays[c] = jnp.exp(cum_end)
            for p in range(pairs):
                ps, vs = lanes_of(p)
                scores = _dot_nt(qe[rs, ps], block_diag(ke[rs, ps], dk_first))
                a_ref[c, p] = jnp.where(keep, scores, 0.0).astype(BF16)
                inc = _dot_tn(v[rs, vs], k_end[:, ps])
                upd_ref[c, p] = jnp.where(same_head, inc, 0.0)
        for p in range(pairs):
            ps, _ = lanes_of(p)
            st = st_ref[p]
            for c in order:
                sin_ref[c, p] = st.astype(BF16)
                st = decays[c][:, ps] * st + upd_ref[c, p]
            st_ref[p] = st
        for c in order:
            rs = rows_of(c)
            for p in range(pairs):
                ps, vs = lanes_of(p)
                o_ref[rs, vs] = (_dot(a_ref[c, p], block_diag(v[rs, vs], dv_first))
                                 + _dot_nt(qe[rs, ps], sin_ref[c, p]))
        return row0

    @pl.when(t < nb)
    def _():
        row0 = scan_block(True)
        ob_ref[pl.ds(row0, tb), :] = o_ref[...]

    @pl.when(t >= nb)
    def _():
        row0 = scan_block(False)
        tot = o_ref[...] + ob_ref[pl.ds(row0, tb), :]
        gr = gr_ref[0]
        for h in range(GLA_HEADS):
            vs = slice(h * GLA_DV, (h + 1) * GLA_DV)
            y = _rms(tot[:, vs], onorm_ref[:, vs])
            gate = gr[:, vs]
            y_ref[0, :, vs] = (y * (gate * jax.nn.sigmoid(gate))).astype(BF16)


def _gla(gq, gk, gv, gfb, gr, up, gbias, onorm, tri, l, *, tb=512):
    b, s, _ = gq.shape
    nb = s // tb
    nc = tb // GLA_CHUNK
    pairs = GLA_HEADS // 2
    assert tri.shape == (2, tb, tb)

    def scan_idx(bi, t):
        return (bi, jnp.where(t < nb, nb - 1 - t, t - nb), 0)

    def fwd_idx(bi, t):
        return (bi, jnp.maximum(t - nb, 0), 0)

    return pl.pallas_call(
        functools.partial(_gla_body, nb=nb, tb=tb),
        grid=(b, 2 * nb),
        in_specs=[pl.BlockSpec((1, tb, GLA_QK_W), scan_idx),
                  pl.BlockSpec((1, tb, GLA_QK_W), scan_idx),
                  pl.BlockSpec((1, tb, GLA_V_W), scan_idx),
                  pl.BlockSpec((1, tb, LANES), scan_idx),
                  pl.BlockSpec((1, tb, GLA_V_W), fwd_idx),
                  _layer(up, l), _layer(gbias, l), _layer(onorm, l), _resident(tri.shape)],
        out_specs=pl.BlockSpec((1, tb, GLA_V_W), fwd_idx),
        out_shape=jax.ShapeDtypeStruct((b, s, GLA_V_W), BF16),
        scratch_shapes=[pltpu.VMEM((s, GLA_V_W), F32),
                        pltpu.VMEM((pairs, 2 * GLA_DV, LANES), F32),
                        pltpu.VMEM((nc, pairs, 2 * GLA_DV, LANES), F32),
                        pltpu.VMEM((nc, pairs, 2 * GLA_DV, LANES), BF16),
                        pltpu.VMEM((nc, pairs, GLA_CHUNK, LANES), BF16),
                        pltpu.VMEM((tb, GLA_V_W), F32)],
        compiler_params=_cparams("arbitrary", "arbitrary"),
        name="gla",
    )(gq, gk, gv, gfb, gr, up, gbias, onorm, tri)


def _gla_tri(tb):
    i = np.arange(tb)
    same = (i[:, None] // GLA_CHUNK) == (i[None, :] // GLA_CHUNK)
    lower = same & (i[None, :] <= i[:, None])
    upper = same & (i[None, :] >= i[:, None])
    return jnp.asarray(np.stack([lower, upper]).astype(np.float32), BF16)


def _mla_attn_body(q_ref, k_ref, v_ref, o_ref, *, tk):
    q = q_ref[0]
    tq = q.shape[0]
    s_len = k_ref.shape[1]
    ones = jnp.ones((tk, LANES), BF16)
    m = jnp.full((tq, 1), -jnp.inf, F32)
    acc = jnp.zeros((tq, MLA_V + LANES), F32)
    for j in range(s_len // tk):
        kb = k_ref[0, j * tk:(j + 1) * tk, :]
        vb = v_ref[0, j * tk:(j + 1) * tk, :]
        s = _dot_nt(q, kb)
        m_new = jnp.maximum(m, jnp.max(s, axis=-1, keepdims=True))
        p = jnp.exp2(s - m_new).astype(BF16)
        acc = jnp.exp2(m - m_new) * acc + _dot(p, jnp.concatenate([vb, ones], axis=1))
        m = m_new
    o_ref[0] = (acc[:, :MLA_V] / acc[:, MLA_V:]).astype(BF16)


def _mla_attention(q, k, v, *, tq=2048, tk=256):
    b, s, _ = q.shape
    assert s % tq == 0 and s % tk == 0
    return pl.pallas_call(
        functools.partial(_mla_attn_body, tk=tk),
        grid=(b, MLA_HEADS, s // tq),
        in_specs=[pl.BlockSpec((1, tq, MLA_SLAB), lambda bi, h, qi: (bi, qi, h)),
                  pl.BlockSpec((1, s, MLA_SLAB), lambda bi, h, qi: (bi, 0, h)),
                  pl.BlockSpec((1, s, MLA_V), lambda bi, h, qi: (bi, 0, h))],
        out_specs=pl.BlockSpec((1, tq, MLA_V), lambda bi, h, qi: (bi, qi, h)),
        out_shape=jax.ShapeDtypeStruct((b, s, MLA_V_W), BF16),
        compiler_params=_cparams("parallel", "parallel", "arbitrary"),
        name="mla_attn",
    )(q, k, v)


def _merge_body(x_ref, yna_ref, ygla_ref, ymla_ref, gt_ref, wbr_ref, wout_ref, o_ref):
    mixed = None
    for i, y_ref in enumerate((yna_ref, ygla_ref, ymla_ref)):
        br = _dot(y_ref[...], wbr_ref[i])
        term = gt_ref[:, i * D_MODEL:(i + 1) * D_MODEL].astype(F32) * br
        mixed = term if mixed is None else mixed + term
    o_ref[...] = x_ref[...] + _dot(mixed.astype(BF16), wout_ref[...])


def _merge(x, yna, ygla, ymla, gt, wbr, wout, l, *, tm=1024):
    n, d = x.shape
    assert n % tm == 0
    tok = lambda c: pl.BlockSpec((tm, c), lambda i: (i, 0))
    return pl.pallas_call(
        _merge_body,
        grid=(n // tm,),
        in_specs=[tok(d), tok(NA_W), tok(GLA_V_W), tok(MLA_V_W), tok(N_BRANCH * d),
                  _layer(wbr, l), _layer(wout, l)],
        out_specs=tok(d),
        out_shape=jax.ShapeDtypeStruct((n, d), F32),
        compiler_params=_cparams("parallel"),
        name="merge",
    )(x, yna, ygla, ymla, gt, wbr, wout)


def _rope_tables(s):
    half = MLA_ROPE // 2
    inv = ROPE_THETA ** (-jnp.arange(half, dtype=F32) / half)
    ang = jnp.arange(s, dtype=F32)[:, None] * inv[None, :]
    cos, sin = jnp.cos(ang), jnp.sin(ang)
    z = jnp.zeros((s, half), F32)
    pad = jnp.zeros((s, LANES - MLA_ROPE), F32)
    return (jnp.concatenate([cos, cos, pad], axis=1),
            jnp.concatenate([-sin, z, pad], axis=1),
            jnp.concatenate([z, sin, pad], axis=1))


def _pack_w_in(w_in):
    nl, d, _ = w_in.shape
    gfb_end = 3 * NA_W + 2 * GLA_QK_W + 2 * GLA_V_W + 2 * GLA_GATE_RANK
    kr_end = gfb_end + MLA_Q_RANK + MLA_KV_RANK + MLA_ROPE
    z = lambda c: jnp.zeros((nl, d, c), BF16)
    kr_start = kr_end - MLA_ROPE
    cols = lambda a, b: w_in[:, :, a:b].astype(BF16)
    latent = jnp.concatenate([cols(C_GFB, gfb_end), z(LANES - 2 * GLA_GATE_RANK),
                              cols(kr_start, kr_end), z(LANES - MLA_ROPE),
                              cols(gfb_end, kr_start)], axis=2)
    return (cols(0, C_GFB), latent, cols(kr_end, w_in.shape[2]))


def kernel(x, ffn1_norm, ffn1_w1, ffn1_w3, ffn1_w2, mix_norm, w_in, na_q_norm, na_k_norm, na_rpb, gla_gf_up, gla_gf_bias, gla_gb_up, gla_gb_bias, gla_out_norm, mla_cq_norm, mla_ckv_norm, mla_w_uq, mla_w_ukv, mla_q_norm, mla_k_norm, w_br_na, w_br_gla, w_br_mla, w_out, ffn2_norm, ffn2_w1, ffn2_w3, ffn2_w2):
    b, s, d = x.shape
    nl = w_in.shape[0]
    n = b * s
    gla_tb = 512

    row = lambda a: a[:, None, :].astype(F32)
    f1 = (row(ffn1_norm), ffn1_w1.astype(BF16), ffn1_w3.astype(BF16), ffn1_w2.astype(BF16))
    f2 = (row(ffn2_norm), ffn2_w1.astype(BF16), ffn2_w3.astype(BF16), ffn2_w2.astype(BF16))
    mixg = row(mix_norm)
    w_in_p = _pack_w_in(w_in)
    na_qg = row(jnp.tile(na_q_norm, (1, NA_HEADS)) * (NA_HEAD_DIM ** -0.5))
    na_kg = row(jnp.tile(na_k_norm, (1, NA_HEADS)))
    head_of = np.arange(NA_W) // NA_HEAD_DIM
    hsum = jnp.asarray((head_of[:, None] == head_of[None, :]).astype(np.float32), BF16)
    na_bias = _na_bias_table(na_rpb)

    up = jnp.zeros((nl, LANES, 2 * GLA_QK_W), F32)
    up = up.at[:, :GLA_GATE_RANK, :GLA_QK_W].set(gla_gf_up)
    up = up.at[:, GLA_GATE_RANK:2 * GLA_GATE_RANK, GLA_QK_W:].set(gla_gb_up).astype(BF16)
    gbias = row(jnp.concatenate([gla_gf_bias, gla_gb_bias], axis=1))
    onorm = row(jnp.tile(gla_out_norm, (1, GLA_HEADS)))
    tri = _gla_tri(gla_tb)

    cqg, ckvg = row(mla_cq_norm), row(mla_ckv_norm)
    wuq = mla_w_uq.reshape(nl, MLA_Q_RANK, MLA_HEADS, MLA_QK_HEAD)
    wuq = jnp.pad(wuq, ((0, 0), (0, 0), (0, 0), (0, MLA_SLAB - MLA_QK_HEAD)))
    wuq = wuq.reshape(nl, MLA_Q_RANK, MLA_HEADS * MLA_SLAB).astype(BF16)
    wukv = mla_w_ukv.reshape(nl, MLA_KV_RANK, MLA_HEADS, 2, MLA_NOPE)
    wukv = jnp.swapaxes(wukv, 2, 3).reshape(nl, MLA_KV_RANK, 2 * MLA_HEADS * MLA_NOPE).astype(BF16)
    q_fold = (MLA_QK_HEAD ** -0.5) * math.log2(math.e)
    mqg = row(jnp.pad(mla_q_norm, ((0, 0), (0, MLA_SLAB - MLA_QK_HEAD))) * q_fold)
    mkgn = row(mla_k_norm[:, :MLA_NOPE])
    mkgr = row(jnp.pad(mla_k_norm[:, MLA_NOPE:], ((0, 0), (0, LANES - MLA_ROPE))))
    cos, sa, sb = _rope_tables(s)

    wbr = jnp.stack([w_br_na, w_br_gla, w_br_mla], axis=1).astype(BF16)
    wout = w_out.astype(BF16)

    xf = x.reshape(n, d).astype(F32)
    seq3 = lambda a: a.reshape(b, s, a.shape[-1])
    for l in range(nl):
        xf = _ffn(xf, *f1, l)
        (naq, nak, nav, gq, gk, gv, gr, gfb, mq, mk, mv, gt) = _inproj(
            xf, mixg, w_in_p, na_qg, na_kg, hsum, cqg, ckvg, wuq, wukv, mqg, mkgn, mkgr, cos, sa, sb, l, seq=s)
        y_na = _na_attention(seq3(naq), seq3(nak), seq3(nav), na_bias, l)
        y_gla = _gla(seq3(gq), seq3(gk), seq3(gv), seq3(gfb), seq3(gr), up, gbias, onorm, tri, l, tb=gla_tb)
        y_mla = _mla_attention(seq3(mq), seq3(mk), seq3(mv))
        xf = _merge(xf, y_na.reshape(n, NA_W), y_gla.reshape(n, GLA_V_W), y_mla.reshape(n, MLA_V_W),
                    gt, wbr, wout, l)
        xf = _ffn(xf, *f2, l)
    return xf.reshape(b, s, d).astype(x.dtype)
```

```python
import functools
import math

import numpy as np
import jax
import jax.numpy as jnp
from jax import lax
from jax.experimental import pallas as pl
from jax.experimental.pallas import tpu as pltpu

F32 = jnp.float32
BF16 = jnp.bfloat16

EPS = 1e-6
D_MODEL = 1024
D_FF = 2816
GRID_W = 64
NA_HEADS = 8
NA_HEAD_DIM = 64
NA_WIN_R = 8
NA_WIN_C = 16
NA_W = NA_HEADS * NA_HEAD_DIM
GLA_HEADS = 4
GLA_DK = 64
GLA_DV = 128
GLA_GATE_RANK = 16
GLA_GATE_TAU = 16.0
GLA_CHUNK = 64
GLA_QK_W = GLA_HEADS * GLA_DK
GLA_V_W = GLA_HEADS * GLA_DV
MLA_HEADS = 4
MLA_Q_RANK = 256
MLA_KV_RANK = 256
MLA_NOPE = 128
MLA_ROPE = 64
MLA_V = 128
MLA_QK_HEAD = MLA_NOPE + MLA_ROPE
MLA_SLAB = 256
MLA_V_W = MLA_HEADS * MLA_V
ROPE_THETA = 10000.0
N_BRANCH = 3

LANES = 128
MXU_DIM = 256
C_NA = 0
C_GQ = 3 * NA_W
C_GK = C_GQ + GLA_QK_W
C_GV = C_GK + GLA_QK_W
C_GR = C_GV + GLA_V_W
C_GFB = C_GR + GLA_V_W
C_KR = C_GFB + LANES
C_CQ = C_KR + LANES
C_CKV = C_CQ + MLA_Q_RANK
C_GATE = C_CKV + MLA_KV_RANK
D_IN_PACKED = C_GATE + N_BRANCH * D_MODEL

VMEM_LIMIT = 56 * 1024 * 1024


def _cparams(*sem):
    return pltpu.CompilerParams(dimension_semantics=sem, vmem_limit_bytes=VMEM_LIMIT)


def _resident(shape):
    nd = len(shape)
    return pl.BlockSpec(shape, lambda *_: (0,) * nd, pipeline_mode=pl.Buffered(1))


def _layer(arr, l):
    nd = arr.ndim - 1
    return pl.BlockSpec((None,) + tuple(arr.shape[1:]), lambda *_: (l,) + (0,) * nd,
                        pipeline_mode=pl.Buffered(1))


def _dot(a, b):
    return jnp.dot(a, b, preferred_element_type=F32)


def _dot_nt(a, b):
    return lax.dot_general(a, b, (((1,), (1,)), ((), ())), preferred_element_type=F32)


def _dot_tn(a, b):
    return lax.dot_general(a, b, (((0,), (0,)), ((), ())), preferred_element_type=F32)


def _rms(x, g):
    ms = jnp.mean(x * x, axis=-1, keepdims=True)
    return x * lax.rsqrt(ms + EPS) * g


def _ffn_body(x_ref, g_ref, w1_ref, w3_ref, w2_ref, o_ref, *, ff_bounds):
    x = x_ref[...]
    h = _rms(x, g_ref[...]).astype(BF16)
    acc = None
    for f0, f1 in zip(ff_bounds[:-1], ff_bounds[1:]):
        a = _dot(h, w1_ref[:, f0:f1])
        b = _dot(h, w3_ref[:, f0:f1])
        u = (a * jax.nn.sigmoid(a) * b).astype(BF16)
        part = _dot(u, w2_ref[f0:f1, :])
        acc = part if acc is None else acc + part
    o_ref[...] = x + 0.5 * acc


def _ffn(x, g, w1, w3, w2, l, *, tm=1024):
    n, d = x.shape
    tiles = D_FF // MXU_DIM
    ff_bounds = (0, 4 * MXU_DIM, 8 * MXU_DIM, D_FF)
    assert D_FF % MXU_DIM == 0 and n % tm == 0
    return pl.pallas_call(
        functools.partial(_ffn_body, ff_bounds=ff_bounds),
        grid=(n // tm,),
        in_specs=[pl.BlockSpec((tm, d), lambda i: (i, 0)),
                  _layer(g, l), _layer(w1, l), _layer(w3, l), _layer(w2, l)],
        out_specs=pl.BlockSpec((tm, d), lambda i: (i, 0)),
        out_shape=jax.ShapeDtypeStruct((n, d), F32),
        compiler_params=_cparams("parallel"),
        name="ffn",
    )(x, g, w1, w3, w2)


def _mla_qkv(cq, ckv, kr, cqg, ckvg, wuq, wukv, qg, kgn, kgr, cos, sa, sb, q_ref, k_ref, v_ref):
    def rope(slab):
        return slab * cos + pltpu.roll(slab, LANES - MLA_ROPE // 2, 1) * sa + pltpu.roll(slab, MLA_ROPE // 2, 1) * sb

    inv_d = 1.0 / MLA_QK_HEAD
    q = _dot(_rms(cq, cqg).astype(BF16), wuq)
    for h in range(MLA_HEADS):
        qh = q[:, h * MLA_SLAB:(h + 1) * MLA_SLAB]
        ss = jnp.sum(qh * qh, axis=-1, keepdims=True) * inv_d
        qn = qh * lax.rsqrt(ss + EPS) * qg
        q_ref[:, h * MLA_SLAB:h * MLA_SLAB + LANES] = qn[:, :LANES].astype(BF16)
        q_ref[:, h * MLA_SLAB + LANES:(h + 1) * MLA_SLAB] = rope(qn[:, LANES:]).astype(BF16)

    kv = _dot(_rms(ckv, ckvg).astype(BF16), wukv)
    ss_r = jnp.sum(kr * kr, axis=-1, keepdims=True)
    kr_rot = rope(kr * kgr)
    for h in range(MLA_HEADS):
        kn = kv[:, h * MLA_NOPE:(h + 1) * MLA_NOPE]
        ss = (jnp.sum(kn * kn, axis=-1, keepdims=True) + ss_r) * inv_d
        inv = lax.rsqrt(ss + EPS)
        k_ref[:, h * MLA_SLAB:h * MLA_SLAB + LANES] = (kn * inv * kgn).astype(BF16)
        k_ref[:, h * MLA_SLAB + LANES:(h + 1) * MLA_SLAB] = (kr_rot * inv).astype(BF16)
    v_ref[...] = kv[:, MLA_HEADS * MLA_NOPE:].astype(BF16)


def _inproj_body(x_ref, g_ref, wa_ref, wb_ref, wd_ref, qg_ref, kg_ref, hsum_ref,
                 cqg_ref, ckvg_ref, wuq_ref, wukv_ref, mqg_ref, mkgn_ref, mkgr_ref, cos_ref, sa_ref, sb_ref,
                 naq_ref, nak_ref, nav_ref, gq_ref, gk_ref, gv_ref, gr_ref, gfb_ref,
                 mq_ref, mk_ref, mv_ref, gt_ref):
    h = _rms(x_ref[...], g_ref[...]).astype(BF16)
    hsum = hsum_ref[...]

    def head_norm(t, gain):
        ss = _dot((t * t).astype(BF16), hsum)
        return t * lax.rsqrt(ss * (1.0 / NA_HEAD_DIM) + EPS) * gain

    na = _dot(h, wa_ref[:, C_NA:C_GQ])
    naq_ref[...] = head_norm(na[:, :NA_W], qg_ref[...]).astype(BF16)
    nak_ref[...] = head_norm(na[:, NA_W:2 * NA_W], kg_ref[...]).astype(BF16)
    nav_ref[...] = na[:, 2 * NA_W:].astype(BF16)
    gla = _dot(h, wa_ref[:, C_GQ:C_GFB])
    gq_ref[...] = gla[:, C_GQ - C_GQ:C_GK - C_GQ]
    gk_ref[...] = gla[:, C_GK - C_GQ:C_GV - C_GQ]
    gv_ref[...] = gla[:, C_GV - C_GQ:C_GR - C_GQ]
    gr_ref[...] = gla[:, C_GR - C_GQ:]
    lat = _dot(h, wb_ref[...])
    gfb_ref[...] = lat[:, :LANES]
    _mla_qkv(lat[:, C_CQ - C_GFB:C_CKV - C_GFB], lat[:, C_CKV - C_GFB:], lat[:, C_KR - C_GFB:C_CQ - C_GFB],
             cqg_ref[...], ckvg_ref[...], wuq_ref[...], wukv_ref[...],
             mqg_ref[...], mkgn_ref[...], mkgr_ref[...], cos_ref[...], sa_ref[...], sb_ref[...],
             mq_ref, mk_ref, mv_ref)
    gt_ref[...] = jax.nn.sigmoid(_dot(h, wd_ref[...])).astype(BF16)


def _inproj(x, g, w_segs, qg, kg, hsum, cqg, ckvg, wuq, wukv, mqg, mkgn, mkgr, cos, sa, sb, l, *, seq, tm=512):
    n, d = x.shape
    widths = [(NA_W, BF16), (NA_W, BF16), (NA_W, BF16),
              (GLA_QK_W, F32), (GLA_QK_W, F32), (GLA_V_W, F32), (GLA_V_W, F32), (LANES, F32),
              (MLA_HEADS * MLA_SLAB, BF16), (MLA_HEADS * MLA_SLAB, BF16), (MLA_V_W, BF16),
              (N_BRANCH * D_MODEL, BF16)]
    out_specs = [pl.BlockSpec((tm, c), lambda i: (i, 0)) for c, _ in widths]
    out_shape = [jax.ShapeDtypeStruct((n, c), dt) for c, dt in widths]
    pos_blocks = seq // tm
    pos = pl.BlockSpec((tm, LANES), lambda i: (i % pos_blocks, 0))
    return pl.pallas_call(
        _inproj_body,
        grid=(n // tm,),
        in_specs=[pl.BlockSpec((tm, d), lambda i: (i, 0)),
                  _layer(g, l), *[_layer(w, l) for w in w_segs],
                  _layer(qg, l), _layer(kg, l), _resident(hsum.shape),
                  _layer(cqg, l), _layer(ckvg, l), _layer(wuq, l), _layer(wukv, l),
                  _layer(mqg, l), _layer(mkgn, l), _layer(mkgr, l), pos, pos, pos],
        out_specs=out_specs,
        out_shape=out_shape,
        compiler_params=_cparams("parallel"),
        name="inproj",
    )(x, g, *w_segs, qg, kg, hsum, cqg, ckvg, wuq, wukv, mqg, mkgn, mkgr, cos, sa, sb)


def _na_body(q_ref, k_ref, v_ref, tab_ref, o_ref, s_ref, p_ref, bias_ref, *, rb, rows):
    kw_tokens = NA_WIN_R * GRID_W
    pairs = NA_HEADS // 2
    unit_rows = 2 * GRID_W
    lane = lax.broadcasted_iota(jnp.int32, (1, LANES), 1)
    first_head = lane < NA_HEAD_DIM
    r_base = pl.program_id(1) * rb

    @pl.when((pl.program_id(0) == 0) & (pl.program_id(1) == 0))
    def _():
        for d in range(NA_WIN_R):
            off = (NA_WIN_R - 1 - d) * GRID_W
            for p in range(pairs):
                bias_ref[d, p] = tab_ref[p, :, off:off + kw_tokens]

    def window(i):
        r = r_base + i
        r0 = jnp.clip(r - NA_WIN_R // 2, 0, rows - NA_WIN_R)
        return r - r0, pl.ds(pl.multiple_of(r0 * GRID_W, GRID_W), kw_tokens)

    for i in range(rb):
        delta, win = window(i)
        kw = k_ref[0, win, :]
        q = q_ref[0, i * GRID_W:(i + 1) * GRID_W, :]
        for p in range(pairs):
            sl = slice(p * LANES, (p + 1) * LANES)
            qs = q[:, sl]
            zero = jnp.zeros_like(qs)
            lhs = jnp.concatenate([jnp.where(first_head, qs, zero), jnp.where(first_head, zero, qs)], axis=0)
            u = i * pairs + p
            s_ref[u * unit_rows:(u + 1) * unit_rows, :] = _dot_nt(lhs, kw[:, sl]) + bias_ref[delta, p]

    s = s_ref[...]
    p_ref[...] = jnp.exp(s - jnp.max(s, axis=-1, keepdims=True)).astype(BF16)

    ones = jnp.ones((kw_tokens, LANES), BF16)
    for i in range(rb):
        _, win = window(i)
        vw = v_ref[0, win, :]
        for p in range(pairs):
            sl = slice(p * LANES, (p + 1) * LANES)
            u = i * pairs + p
            r = _dot(p_ref[u * unit_rows:(u + 1) * unit_rows, :], jnp.concatenate([vw[:, sl], ones], axis=1))
            o = r[:, :LANES] / r[:, LANES:]
            o_ref[0, i * GRID_W:(i + 1) * GRID_W, sl] = jnp.where(first_head, o[:GRID_W], o[GRID_W:]).astype(BF16)


def _na_attention(q, k, v, bias, l, *, rb=8):
    b, s, w = q.shape
    rows = s // GRID_W
    assert rows >= NA_WIN_R and rows % rb == 0
    units = rb * (NA_HEADS // 2) * 2 * GRID_W
    return pl.pallas_call(
        functools.partial(_na_body, rb=rb, rows=rows),
        grid=(b, rows // rb),
        in_specs=[pl.BlockSpec((1, rb * GRID_W, w), lambda bi, ri: (bi, ri, 0)),
                  pl.BlockSpec((1, s, w), lambda bi, ri: (bi, 0, 0)),
                  pl.BlockSpec((1, s, w), lambda bi, ri: (bi, 0, 0)),
                  _layer(bias, l)],
        out_specs=pl.BlockSpec((1, rb * GRID_W, w), lambda bi, ri: (bi, ri, 0)),
        out_shape=jax.ShapeDtypeStruct((b, s, w), BF16),
        scratch_shapes=[pltpu.VMEM((units, NA_WIN_R * GRID_W), F32),
                        pltpu.VMEM((units, NA_WIN_R * GRID_W), BF16),
                        pltpu.VMEM((NA_WIN_R, NA_HEADS // 2, 2 * GRID_W, NA_WIN_R * GRID_W), F32)],
        compiler_params=_cparams("arbitrary", "arbitrary"),
        name="na_attn",
    )(q, k, v, bias)


def _na_bias_table(rpb):
    nl = rpb.shape[0]
    rel_rows = 2 * NA_WIN_R - 1
    qc = np.arange(GRID_W)
    kc = np.arange(GRID_W)
    dc = np.clip(kc[None, :] - qc[:, None] + (NA_WIN_C - 1), 0, 2 * NA_WIN_C - 2)
    c0 = np.clip(qc - NA_WIN_C // 2, 0, GRID_W - NA_WIN_C)
    ok = (kc[None, :] >= c0[:, None]) & (kc[None, :] < c0[:, None] + NA_WIN_C)
    onehot = jnp.asarray((np.arange(2 * NA_WIN_C - 1)[:, None, None] == dc[None]).astype(np.float32))
    tab = jnp.einsum("lhrc,cqk->lhqrk", rpb.astype(F32), onehot, precision=lax.Precision.HIGHEST)
    tab = jnp.where(ok[None, None, :, None, :], tab, -1e30)
    tab = tab.reshape(nl, NA_HEADS // 2, 2 * GRID_W, rel_rows * GRID_W)
    return jnp.pad(tab, ((0, 0), (0, 0), (0, 0), (0, GRID_W)))


def _gla_body(gq_ref, gk_ref, gv_ref, gfb_ref, gr_ref, up_ref, gbias_ref, onorm_ref, tri_ref,
              y_ref, ob_ref, st_ref, upd_ref, sin_ref, a_ref, o_ref, *, nb, tb):
    t = pl.program_id(1)
    nc = tb // GLA_CHUNK
    pairs = GLA_HEADS // 2
    pair_dv = 2 * GLA_DV
    dk_first = lax.broadcasted_iota(jnp.int32, (1, LANES), 1) < GLA_DK
    dv_first = lax.broadcasted_iota(jnp.int32, (1, pair_dv), 1) < GLA_DV
    ci = lax.broadcasted_iota(jnp.int32, (GLA_CHUNK, LANES), 0)
    cj = lax.broadcasted_iota(jnp.int32, (GLA_CHUNK, LANES), 1) % GLA_CHUNK
    same_head = (lax.broadcasted_iota(jnp.int32, (pair_dv, LANES), 0) < GLA_DV) == dk_first

    @pl.when((t == 0) | (t == nb))
    def _():
        st_ref[...] = jnp.zeros_like(st_ref)

    def scan_block(bwd):
        blk = (nb - 1 - t) if bwd else (t - nb)
        row0 = pl.multiple_of(blk * tb, tb)
        gsl = slice(GLA_QK_W, 2 * GLA_QK_W) if bwd else slice(0, GLA_QK_W)
        x = _dot(gfb_ref[0].astype(BF16), up_ref[:, gsl]) + gbias_ref[:, gsl]
        g = (jnp.minimum(x, 0.0) - jnp.log(1.0 + jnp.exp(-jnp.abs(x)))) * (1.0 / GLA_GATE_TAU)
        g_hi = g.astype(BF16)
        g_lo = (g - g_hi.astype(F32)).astype(BF16)
        tri = tri_ref[1 if bwd else 0]
        cum = _dot(tri, g_hi) + _dot(tri, g_lo)
        k = gk_ref[0]
        qe = (gq_ref[0] * (GLA_DK ** -0.5) * jnp.exp(cum)).astype(BF16)
        ke = (k * jnp.exp(-cum)).astype(BF16)
        v = gv_ref[0].astype(BF16)
        keep = (cj > ci) if bwd else (cj <= ci)
        order = list(range(nc - 1, -1, -1) if bwd else range(nc))

        def rows_of(c):
            return slice(c * GLA_CHUNK, (c + 1) * GLA_CHUNK)

        def lanes_of(p):
            return slice(p * LANES, (p + 1) * LANES), slice(p * pair_dv, (p + 1) * pair_dv)

        def block_diag(x, first):
            zero = jnp.zeros_like(x)
            return jnp.concatenate([jnp.where(first, x, zero), jnp.where(first, zero, x)], axis=0)

        decays = {}
        for c in order:
            rs = rows_of(c)
            edge = c * GLA_CHUNK if bwd else (c + 1) * GLA_CHUNK - 1
            cum_end = cum[edge:edge + 1, :]
            k_end = (k[rs] * jnp.exp(cum_end - cum[rs])).astype(BF16)
            decays[c] = jnp.exp(cum_end)
            for p in range(pairs):
                ps, vs = lanes_of(p)
                scores = _dot_nt(qe[rs, ps], block_diag(ke[rs, ps], dk_first))
                a_ref[c, p] = jnp.where(keep, scores, 0.0).astype(BF16)
                inc = _dot_tn(v[rs, vs], k_end[:, ps])
                upd_ref[c, p] = jnp.where(same_head, inc, 0.0)
        for p in range(pairs):
            ps, _ = lanes_of(p)
            st = st_ref[p]
            for c in order:
                sin_ref[c, p] = st.astype(BF16)
                st = decays[c][:, ps] * st + upd_ref[c, p]
            st_ref[p] = st
        for c in order:
            rs = rows_of(c)
            for p in range(pairs):
                ps, vs = lanes_of(p)
                o_ref[rs, vs] = (_dot(a_ref[c, p], block_diag(v[rs, vs], dv_first))
                                 + _dot_nt(qe[rs, ps], sin_ref[c, p]))
        return row0

    @pl.when(t < nb)
    def _():
        row0 = scan_block(True)
        ob_ref[pl.ds(row0, tb), :] = o_ref[...]

    @pl.when(t >= nb)
    def _():
        row0 = scan_block(False)
        tot = o_ref[...] + ob_ref[pl.ds(row0, tb), :]
        gr = gr_ref[0]
        for h in range(GLA_HEADS):
            vs = slice(h * GLA_DV, (h + 1) * GLA_DV)
            y = _rms(tot[:, vs], onorm_ref[:, vs])
            gate = gr[:, vs]
            y_ref[0, :, vs] = (y * (gate * jax.nn.sigmoid(gate))).astype(BF16)


def _gla(gq, gk, gv, gfb, gr, up, gbias, onorm, tri, l, *, tb=512):
    b, s, _ = gq.shape
    nb = s // tb
    nc = tb // GLA_CHUNK
    pairs = GLA_HEADS // 2
    assert tri.shape == (2, tb, tb)

    def scan_idx(bi, t):
        return (bi, jnp.where(t < nb, nb - 1 - t, t - nb), 0)

    def fwd_idx(bi, t):
        return (bi, jnp.maximum(t - nb, 0), 0)

    return pl.pallas_call(
        functools.partial(_gla_body, nb=nb, tb=tb),
        grid=(b, 2 * nb),
        in_specs=[pl.BlockSpec((1, tb, GLA_QK_W), scan_idx),
                  pl.BlockSpec((1, tb, GLA_QK_W), scan_idx),
                  pl.BlockSpec((1, tb, GLA_V_W), scan_idx),
                  pl.BlockSpec((1, tb, LANES), scan_idx),
                  pl.BlockSpec((1, tb, GLA_V_W), fwd_idx),
                  _layer(up, l), _layer(gbias, l), _layer(onorm, l), _resident(tri.shape)],
        out_specs=pl.BlockSpec((1, tb, GLA_V_W), fwd_idx),
        out_shape=jax.ShapeDtypeStruct((b, s, GLA_V_W), BF16),
        scratch_shapes=[pltpu.VMEM((s, GLA_V_W), F32),
                        pltpu.VMEM((pairs, 2 * GLA_DV, LANES), F32),
                        pltpu.VMEM((nc, pairs, 2 * GLA_DV, LANES), F32),
                        pltpu.VMEM((nc, pairs, 2 * GLA_DV, LANES), BF16),
                        pltpu.VMEM((nc, pairs, GLA_CHUNK, LANES), BF16),
                        pltpu.VMEM((tb, GLA_V_W), F32)],
        compiler_params=_cparams("arbitrary", "arbitrary"),
        name="gla",
    )(gq, gk, gv, gfb, gr, up, gbias, onorm, tri)


def _gla_tri(tb):
    i = np.arange(tb)
    same = (i[:, None] // GLA_CHUNK) == (i[None, :] // GLA_CHUNK)
    lower = same & (i[None, :] <= i[:, None])
    upper = same & (i[None, :] >= i[:, None])
    return jnp.asarray(np.stack([lower, upper]).astype(np.float32), BF16)


def _mla_attn_body(q_ref, k_ref, v_ref, o_ref, *, tk):
    q = q_ref[0]
    tq = q.shape[0]
    s_len = k_ref.shape[1]
    ones = jnp.ones((tk, LANES), BF16)
    m = jnp.full((tq, 1), -jnp.inf, F32)
    acc = jnp.zeros((tq, MLA_V + LANES), F32)
    for j in range(s_len // tk):
        kb = k_ref[0, j * tk:(j + 1) * tk, :]
        vb = v_ref[0, j * tk:(j + 1) * tk, :]
        s = _dot_nt(q, kb)
        m_new = jnp.maximum(m, jnp.max(s, axis=-1, keepdims=True))
        p = jnp.exp2(s - m_new).astype(BF16)
        acc = jnp.exp2(m - m_new) * acc + _dot(p, jnp.concatenate([vb, ones], axis=1))
        m = m_new
    o_ref[0] = (acc[:, :MLA_V] / acc[:, MLA_V:]).astype(BF16)


def _mla_attention(q, k, v, *, tq=4096, tk=256):
    b, s, _ = q.shape
    assert s % tq == 0 and s % tk == 0
    return pl.pallas_call(
        functools.partial(_mla_attn_body, tk=tk),
        grid=(b, MLA_HEADS, s // tq),
        in_specs=[pl.BlockSpec((1, tq, MLA_SLAB), lambda bi, h, qi: (bi, qi, h)),
                  pl.BlockSpec((1, s, MLA_SLAB), lambda bi, h, qi: (bi, 0, h)),
                  pl.BlockSpec((1, s, MLA_V), lambda bi, h, qi: (bi, 0, h))],
        out_specs=pl.BlockSpec((1, tq, MLA_V), lambda bi, h, qi: (bi, qi, h)),
        out_shape=jax.ShapeDtypeStruct((b, s, MLA_V_W), BF16),
        compiler_params=_cparams("parallel", "parallel", "arbitrary"),
        name="mla_attn",
    )(q, k, v)


def _merge_body(x_ref, yna_ref, ygla_ref, ymla_ref, gt_ref, wbr_ref, wout_ref, o_ref):
    mixed = None
    for i, y_ref in enumerate((yna_ref, ygla_ref, ymla_ref)):
        br = _dot(y_ref[...], wbr_ref[i])
        term = gt_ref[:, i * D_MODEL:(i + 1) * D_MODEL].astype(F32) * br
        mixed = term if mixed is None else mixed + term
    o_ref[...] = x_ref[...] + _dot(mixed.astype(BF16), wout_ref[...])


def _merge(x, yna, ygla, ymla, gt, wbr, wout, l, *, tm=1024):
    n, d = x.shape
    assert n % tm == 0
    tok = lambda c: pl.BlockSpec((tm, c), lambda i: (i, 0))
    return pl.pallas_call(
        _merge_body,
        grid=(n // tm,),
        in_specs=[tok(d), tok(NA_W), tok(GLA_V_W), tok(MLA_V_W), tok(N_BRANCH * d),
                  _layer(wbr, l), _layer(wout, l)],
        out_specs=tok(d),
        out_shape=jax.ShapeDtypeStruct((n, d), F32),
        compiler_params=_cparams("parallel"),
        name="merge",
    )(x, yna, ygla, ymla, gt, wbr, wout)


def _rope_tables(s):
    half = MLA_ROPE // 2
    inv = ROPE_THETA ** (-jnp.arange(half, dtype=F32) / half)
    ang = jnp.arange(s, dtype=F32)[:, None] * inv[None, :]
    cos, sin = jnp.cos(ang), jnp.sin(ang)
    z = jnp.zeros((s, half), F32)
    pad = jnp.zeros((s, LANES - MLA_ROPE), F32)
    return (jnp.concatenate([cos, cos, pad], axis=1),
            jnp.concatenate([-sin, z, pad], axis=1),
            jnp.concatenate([z, sin, pad], axis=1))


def _pack_w_in(w_in):
    nl, d, _ = w_in.shape
    gfb_end = 3 * NA_W + 2 * GLA_QK_W + 2 * GLA_V_W + 2 * GLA_GATE_RANK
    kr_end = gfb_end + MLA_Q_RANK + MLA_KV_RANK + MLA_ROPE
    z = lambda c: jnp.zeros((nl, d, c), BF16)
    kr_start = kr_end - MLA_ROPE
    cols = lambda a, b: w_in[:, :, a:b].astype(BF16)
    latent = jnp.concatenate([cols(C_GFB, gfb_end), z(LANES - 2 * GLA_GATE_RANK),
                              cols(kr_start, kr_end), z(LANES - MLA_ROPE),
                              cols(gfb_end, kr_start)], axis=2)
    return (cols(0, C_GFB), latent, cols(kr_end, w_in.shape[2]))


def kernel(x, ffn1_norm, ffn1_w1, ffn1_w3, ffn1_w2, mix_norm, w_in, na_q_norm, na_k_norm, na_rpb, gla_gf_up, gla_gf_bias, gla_gb_up, gla_gb_bias, gla_out_norm, mla_cq_norm, mla_ckv_norm, mla_w_uq, mla_w_ukv, mla_q_norm, mla_k_norm, w_br_na, w_br_gla, w_br_mla, w_out, ffn2_norm, ffn2_w1, ffn2_w3, ffn2_w2):
    b, s, d = x.shape
    nl = w_in.shape[0]
    n = b * s
    gla_tb = 512

    row = lambda a: a[:, None, :].astype(F32)
    f1 = (row(ffn1_norm), ffn1_w1.astype(BF16), ffn1_w3.astype(BF16), ffn1_w2.astype(BF16))
    f2 = (row(ffn2_norm), ffn2_w1.astype(BF16), ffn2_w3.astype(BF16), ffn2_w2.astype(BF16))
    mixg = row(mix_norm)
    w_in_p = _pack_w_in(w_in)
    na_qg = row(jnp.tile(na_q_norm, (1, NA_HEADS)) * (NA_HEAD_DIM ** -0.5))
    na_kg = row(jnp.tile(na_k_norm, (1, NA_HEADS)))
    head_of = np.arange(NA_W) // NA_HEAD_DIM
    hsum = jnp.asarray((head_of[:, None] == head_of[None, :]).astype(np.float32), BF16)
    na_bias = _na_bias_table(na_rpb)

    up = jnp.zeros((nl, LANES, 2 * GLA_QK_W), F32)
    up = up.at[:, :GLA_GATE_RANK, :GLA_QK_W].set(gla_gf_up)
    up = up.at[:, GLA_GATE_RANK:2 * GLA_GATE_RANK, GLA_QK_W:].set(gla_gb_up).astype(BF16)
    gbias = row(jnp.concatenate([gla_gf_bias, gla_gb_bias], axis=1))
    onorm = row(jnp.tile(gla_out_norm, (1, GLA_HEADS)))
    tri = _gla_tri(gla_tb)

    cqg, ckvg = row(mla_cq_norm), row(mla_ckv_norm)
    wuq = mla_w_uq.reshape(nl, MLA_Q_RANK, MLA_HEADS, MLA_QK_HEAD)
    wuq = jnp.pad(wuq, ((0, 0), (0, 0), (0, 0), (0, MLA_SLAB - MLA_QK_HEAD)))
    wuq = wuq.reshape(nl, MLA_Q_RANK, MLA_HEADS * MLA_SLAB).astype(BF16)
    wukv = mla_w_ukv.reshape(nl, MLA_KV_RANK, MLA_HEADS, 2, MLA_NOPE)
    wukv = jnp.swapaxes(wukv, 2, 3).reshape(nl, MLA_KV_RANK, 2 * MLA_HEADS * MLA_NOPE).astype(BF16)
    q_fold = (MLA_QK_HEAD ** -0.5) * math.log2(math.e)
    mqg = row(jnp.pad(mla_q_norm, ((0, 0), (0, MLA_SLAB - MLA_QK_HEAD))) * q_fold)
    mkgn = row(mla_k_norm[:, :MLA_NOPE])
    mkgr = row(jnp.pad(mla_k_norm[:, MLA_NOPE:], ((0, 0), (0, LANES - MLA_ROPE))))
    cos, sa, sb = _rope_tables(s)

    wbr = jnp.stack([w_br_na, w_br_gla, w_br_mla], axis=1).astype(BF16)
    wout = w_out.astype(BF16)

    xf = x.reshape(n, d).astype(F32)
    seq3 = lambda a: a.reshape(b, s, a.shape[-1])
    for l in range(nl):
        xf = _ffn(xf, *f1, l)
        (naq, nak, nav, gq, gk, gv, gr, gfb, mq, mk, mv, gt) = _inproj(
            xf, mixg, w_in_p, na_qg, na_kg, hsum, cqg, ckvg, wuq, wukv, mqg, mkgn, mkgr, cos, sa, sb, l, seq=s)
        y_na = _na_attention(seq3(naq), seq3(nak), seq3(nav), na_bias, l)
        y_gla = _gla(seq3(gq), seq3(gk), seq3(gv), seq3(gfb), seq3(gr), up, gbias, onorm, tri, l, tb=gla_tb)
        y_mla = _mla_attention(seq3(mq), seq3(mk), seq3(mv))
        xf = _merge(xf, y_na.reshape(n, NA_W), y_gla.reshape(n, GLA_V_W), y_mla.reshape(n, MLA_V_W),
                    gt, wbr, wout, l)
        xf = _ffn(xf, *f2, l)
    return xf.reshape(b, s, d).astype(x.dtype)
```
